```python
import math
import jax, jax.numpy as jnp
from jax import lax
import numpy as np


D_MODEL = 2048
BATCH = 1
SEQ = 8192
DEPTH = 4

N_MIXERS = 4
N_REPEAT = max(DEPTH // N_MIXERS, 1)
BLK = 128
NEG = -1e30
EPS = 1e-6
F32 = jnp.float32

REL_BUCKETS = 32
REL_MAX_DIST = 2048
REL_HEADS = 32

MLA_HEADS = 16
MLA_Q_RANK = 512
MLA_KV_RANK = 512
MLA_NOPE = 128
MLA_ROPE = 64
MLA_V = 128
ROPE_THETA = 10000.0

SWA_Q_HEADS = 32
SWA_KV_HEADS = 4
SWA_HEAD_DIM = 64
SWA_WINDOW = 128

DIL_HEADS = 32
DIL_HEAD_DIM = 64
DIL_PAIRS = ((128, 1), (512, 4), (2048, 16))

FOX_HEADS = 16
FOX_HEAD_DIM = 128

D_FF = 5632
CONV_WIDTH = 3

kernel_name = 'hybrid_interleaved_mla_swa_dilated_fox'


def _rmsnorm(x, g):
    xf = x.astype(F32)
    y = xf * lax.rsqrt(jnp.mean(xf * xf, axis=-1, keepdims=True) + EPS)
    return (y * g.astype(F32)).astype(x.dtype)


def _rope(x, positions):
    half = x.shape[-1] // 2
    inv_freq = ROPE_THETA ** (-jnp.arange(half, dtype=F32) / half)
    ang = positions.astype(F32)[:, :, None] * inv_freq
    ang = ang.reshape(ang.shape[:2] + (1,) * (x.ndim - 3) + (half,))
    cos, sin = jnp.cos(ang), jnp.sin(ang)
    x1 = x[..., :half].astype(F32)
    x2 = x[..., half:].astype(F32)
    return jnp.concatenate([x1 * cos - x2 * sin, x2 * cos + x1 * sin], axis=-1).astype(x.dtype)


def _t5_bucket(n):
    exact = REL_BUCKETS // 2
    nf = jnp.maximum(n, 1).astype(F32)
    large = exact + (jnp.log(nf / exact) / math.log(REL_MAX_DIST / exact) * (REL_BUCKETS - exact)).astype(jnp.int32)
    return jnp.where(n < exact, n, jnp.minimum(large, REL_BUCKETS - 1))


def _banded_stats(q, k, v, bias, max_dist):
    n, L, hk, g, dh = q.shape
    nb = L // BLK
    qb = q.reshape(n, nb, BLK, hk, g, dh)

    def band(a):
        ab = a.reshape(n, nb, BLK, hk, dh)
        prev = jnp.pad(ab, ((0, 0), (1, 0), (0, 0), (0, 0), (0, 0)))[:, :-1]
        return jnp.concatenate([prev, ab], axis=2)

    kb, vb = band(k), band(v)
    s = jnp.einsum('nbqhgd,nbkhd->nbhgqk', qb, kb, preferred_element_type=F32)
    qi = jnp.arange(BLK)[:, None]
    kj = jnp.arange(2 * BLK)[None, :]
    dist = qi + BLK - kj
    in_seq = jnp.arange(nb)[:, None, None] * BLK + kj - BLK >= 0
    valid = (dist >= 0) & (dist <= max_dist) & in_seq
    b = bias.astype(F32)[:, :, jnp.clip(dist, 0, max_dist)]
    s = jnp.where(valid[None, :, None, None], s + b, NEG)
    m = jnp.max(s, axis=-1)
    p = jnp.exp(s - m[..., None])
    l = jnp.sum(p, axis=-1)
    acc = jnp.einsum('nbhgqk,nbkhd->nbqhgd', p.astype(v.dtype), vb, preferred_element_type=F32)
    to_seq = lambda a: jnp.moveaxis(a, -1, 2).reshape(n, L, hk, g)
    return to_seq(m), to_seq(l), acc.reshape(n, L, hk, g, dh)


def _mla(h, positions, w_in, g_q, g_kv, w_qb, w_kvb, w_o):
    bsz, t, _ = h.shape
    qk_dim = MLA_NOPE + MLA_ROPE
    lat = h @ w_in
    c_q = _rmsnorm(lat[..., :MLA_Q_RANK], g_q)
    c_kv = _rmsnorm(lat[..., MLA_Q_RANK:MLA_Q_RANK + MLA_KV_RANK], g_kv)
    k_rope = _rope(lat[..., MLA_Q_RANK + MLA_KV_RANK:], positions)
    q = (c_q @ w_qb).reshape(bsz, t, MLA_HEADS, qk_dim) * qk_dim ** -0.5
    q_nope = q[..., :MLA_NOPE]
    q_rope = _rope(q[..., MLA_NOPE:], positions)
    kv = (c_kv @ w_kvb).reshape(bsz, t, MLA_HEADS, MLA_NOPE + MLA_V)
    k_nope, v = kv[..., :MLA_NOPE], kv[..., MLA_NOPE:]
    nb = t // BLK
    blocks = lambda a: a.reshape((bsz, nb, BLK) + a.shape[2:]).swapaxes(0, 1)
    kpos = jnp.arange(t)

    def block(args):
        qn, qr, i = args
        s = (jnp.einsum('bqhd,bkhd->bhqk', qn, k_nope, preferred_element_type=F32)
             + jnp.einsum('bqhd,bkd->bhqk', qr, k_rope, preferred_element_type=F32))
        qpos = i * BLK + jnp.arange(BLK)
        s = jnp.where(kpos[None, :] <= qpos[:, None], s, NEG)
        p = jax.nn.softmax(s, axis=-1)
        return jnp.einsum('bhqk,bkhd->bqhd', p.astype(v.dtype), v, preferred_element_type=F32)

    o = lax.map(block, (blocks(q_nope), blocks(q_rope), jnp.arange(nb)))
    o = o.swapaxes(0, 1).reshape(bsz, t, MLA_HEADS * MLA_V).astype(h.dtype)
    return o @ w_o


def _swa_sinks(h, rel_bias, w_qkv, b_qkv, sinks, w_o, b_o):
    bsz, t, _ = h.shape
    grp = SWA_Q_HEADS // SWA_KV_HEADS
    nq = SWA_Q_HEADS * SWA_HEAD_DIM
    nk = SWA_KV_HEADS * SWA_HEAD_DIM
    qkv = h @ w_qkv + b_qkv
    q = qkv[..., :nq].reshape(bsz, t, SWA_KV_HEADS, grp, SWA_HEAD_DIM) * SWA_HEAD_DIM ** -0.5
    k = qkv[..., nq:nq + nk].reshape(bsz, t, SWA_KV_HEADS, SWA_HEAD_DIM)
    v = qkv[..., nq + nk:].reshape(bsz, t, SWA_KV_HEADS, SWA_HEAD_DIM)
    bias = rel_bias[_t5_bucket(jnp.arange(SWA_WINDOW))].T.reshape(SWA_KV_HEADS, grp, SWA_WINDOW)
    m, l, acc = _banded_stats(q, k, v, bias, SWA_WINDOW - 1)
    sink = sinks.astype(F32).reshape(SWA_KV_HEADS, grp)
    m2 = jnp.maximum(m, sink)
    a = jnp.exp(m - m2)
    den = l * a + jnp.exp(sink - m2)
    o = acc * (a / den)[..., None]
    return o.reshape(bsz, t, nq).astype(h.dtype) @ w_o + b_o


def _dilated_attention(h, rel_bias, w_qkv, w_o):
    bsz, t, _ = h.shape
    q, k, v = jnp.split(h @ w_qkv, 3, axis=-1)
    q = q.reshape(bsz, t, DIL_HEADS, 1, DIL_HEAD_DIM) * DIL_HEAD_DIM ** -0.5
    k = k.reshape(bsz, t, DIL_HEADS, DIL_HEAD_DIM)
    v = v.reshape(bsz, t, DIL_HEADS, DIL_HEAD_DIM)
    ms, ls, accs = [], [], []
    for window, rate in DIL_PAIRS:
        n_keys = window // rate
        ts = t // rate
        lp = -(-ts // BLK) * BLK

        def to_sub(a):
            a = a.reshape((bsz, ts, rate) + a.shape[2:]).swapaxes(1, 2).reshape((bsz * rate, ts) + a.shape[2:])
            return jnp.pad(a, ((0, 0), (0, lp - ts)) + ((0, 0),) * (a.ndim - 2))

        def from_sub(a):
            a = a[:, :ts]
            return a.reshape((bsz, rate, ts) + a.shape[2:]).swapaxes(1, 2).reshape((bsz, t) + a.shape[2:])

        bias = rel_bias[_t5_bucket(rate * jnp.arange(n_keys + 1))].T[:, None, :]
        m, l, acc = _banded_stats(to_sub(q), to_sub(k), to_sub(v), bias, n_keys)
        ms.append(from_sub(m))
        ls.append(from_sub(l))
        accs.append(from_sub(acc))
    m = jnp.stack(ms)
    l = jnp.stack(ls)
    acc = jnp.stack(accs)
    wgt = jnp.exp(m - jnp.max(m, axis=0))
    o = jnp.sum(acc * wgt[..., None], axis=0) / jnp.sum(l * wgt, axis=0)[..., None]
    return o.reshape(bsz, t, DIL_HEADS * DIL_HEAD_DIM).astype(h.dtype) @ w_o


def _forgetting_attention(h, w_in, b_f, w_o):
    bsz, t, _ = h.shape
    hd = FOX_HEADS * FOX_HEAD_DIM
    proj = h @ w_in
    q = proj[..., :hd].reshape(bsz, t, FOX_HEADS, FOX_HEAD_DIM) * FOX_HEAD_DIM ** -0.5
    k = proj[..., hd:2 * hd].reshape(bsz, t, FOX_HEADS, FOX_HEAD_DIM)
    v = proj[..., 2 * hd:3 * hd].reshape(bsz, t, FOX_HEADS, FOX_HEAD_DIM)
    log_f = jax.nn.log_sigmoid(proj[..., 3 * hd:].astype(F32) + b_f.astype(F32))
    cum = jnp.cumsum(log_f, axis=1).transpose(0, 2, 1)
    nb = t // BLK
    q_blocks = q.reshape(bsz, nb, BLK, FOX_HEADS, FOX_HEAD_DIM).swapaxes(0, 1)
    c_blocks = cum.reshape(bsz, FOX_HEADS, nb, BLK).transpose(2, 0, 1, 3)
    kpos = jnp.arange(t)

    def block(args):
        qb, cb, i = args
        s = jnp.einsum('bqhd,bkhd->bhqk', qb, k, preferred_element_type=F32)
        s = s + cb[..., None] - cum[:, :, None, :]
        qpos = i * BLK + jnp.arange(BLK)
        s = jnp.where(kpos[None, :] <= qpos[:, None], s, NEG)
        p = jax.nn.softmax(s, axis=-1)
        return jnp.einsum('bhqk,bkhd->bqhd', p.astype(v.dtype), v, preferred_element_type=F32)

    o = lax.map(block, (q_blocks, c_blocks, jnp.arange(nb)))
    o = o.swapaxes(0, 1).reshape(bsz, t, hd).astype(h.dtype)
    return o @ w_o


def _conv_ffn(h, w_in, conv_w, conv_b, w_out):
    t = h.shape[1]
    u = h @ w_in
    up = jnp.pad(u, ((0, 0), (CONV_WIDTH - 1, 0), (0, 0)))
    c = conv_b + up[:, 0:t] * conv_w[0]
    for j in range(1, CONV_WIDTH):
        c = c + up[:, j:j + t] * conv_w[j]
    gate, val = jnp.split(c, 2, axis=-1)
    return (jax.nn.silu(gate) * val) @ w_out


def setup_inputs(seed: int = 0) -> dict:
    key = jax.random.key(seed)
    ks = iter(jax.random.split(key, 32))
    R, D = N_REPEAT, D_MODEL

    def w(shape, fan_in):
        return jax.random.normal(next(ks), shape, F32) * fan_in ** -0.5

    def gain(shape):
        return 1.0 + 0.02 * jax.random.normal(next(ks), shape, F32)

    def small(shape, scale):
        return scale * jax.random.normal(next(ks), shape, F32)

    x = jax.random.normal(next(ks), (BATCH, SEQ, D), F32)
    offset = jax.random.randint(next(ks), (BATCH, 1), 0, 1024, dtype=jnp.int32)
    positions = offset + jnp.arange(SEQ, dtype=jnp.int32)[None, :]
    swa_qkv = (SWA_Q_HEADS + 2 * SWA_KV_HEADS) * SWA_HEAD_DIM
    return {
        'x': x,
        'positions': positions,
        'rel_bias': small((REL_BUCKETS, REL_HEADS), 0.3),
        'norm_mix': gain((DEPTH, D)),
        'norm_ffn': gain((DEPTH, D)),
        'mla_w_in': w((R, D, MLA_Q_RANK + MLA_KV_RANK + MLA_ROPE), D),
        'mla_g_q': gain((R, MLA_Q_RANK)),
        'mla_g_kv': gain((R, MLA_KV_RANK)),
        'mla_w_qb': w((R, MLA_Q_RANK, MLA_HEADS * (MLA_NOPE + MLA_ROPE)), MLA_Q_RANK),
        'mla_w_kvb': w((R, MLA_KV_RANK, MLA_HEADS * (MLA_NOPE + MLA_V)), MLA_KV_RANK),
        'mla_w_o': w((R, MLA_HEADS * MLA_V, D), MLA_HEADS * MLA_V),
        'swa_w_qkv': w((R, D, swa_qkv), D),
        'swa_b_qkv': small((R, swa_qkv), 0.02),
        'swa_sinks': small((R, SWA_Q_HEADS), 0.5),
        'swa_w_o': w((R, SWA_Q_HEADS * SWA_HEAD_DIM, D), SWA_Q_HEADS * SWA_HEAD_DIM),
        'swa_b_o': small((R, D), 0.02),
        'dil_w_qkv': w((R, D, 3 * DIL_HEADS * DIL_HEAD_DIM), D),
        'dil_w_o': w((R, DIL_HEADS * DIL_HEAD_DIM, D), DIL_HEADS * DIL_HEAD_DIM),
        'fox_w_in': w((R, D, 3 * FOX_HEADS * FOX_HEAD_DIM + FOX_HEADS), D),
        'fox_b_f': 2.0 + small((R, FOX_HEADS), 0.5),
        'fox_w_o': w((R, FOX_HEADS * FOX_HEAD_DIM, D), FOX_HEADS * FOX_HEAD_DIM),
        'ffn_w_in': w((DEPTH, D, 2 * D_FF), D),
        'ffn_conv_w': w((DEPTH, CONV_WIDTH, 2 * D_FF), CONV_WIDTH),
        'ffn_conv_b': small((DEPTH, 2 * D_FF), 0.01),
        'ffn_w_out': w((DEPTH, D_FF, D), D_FF),
        'final_norm': gain((D,)),
    }


def reference(x, positions, rel_bias, norm_mix, norm_ffn,
              mla_w_in, mla_g_q, mla_g_kv, mla_w_qb, mla_w_kvb, mla_w_o,
              swa_w_qkv, swa_b_qkv, swa_sinks, swa_w_o, swa_b_o,
              dil_w_qkv, dil_w_o,
              fox_w_in, fox_b_f, fox_w_o,
              ffn_w_in, ffn_conv_w, ffn_conv_b, ffn_w_out,
              final_norm):
    h = x
    for i in range(DEPTH):
        kind = i % N_MIXERS
        r = i // N_MIXERS
        a = _rmsnorm(h, norm_mix[i])
        if kind == 0:
            y = _mla(a, positions, mla_w_in[r], mla_g_q[r], mla_g_kv[r], mla_w_qb[r], mla_w_kvb[r], mla_w_o[r])
        elif kind == 1:
            y = _swa_sinks(a, rel_bias, swa_w_qkv[r], swa_b_qkv[r], swa_sinks[r], swa_w_o[r], swa_b_o[r])
        elif kind == 2:
            y = _dilated_attention(a, rel_bias, dil_w_qkv[r], dil_w_o[r])
        else:
            y = _forgetting_attention(a, fox_w_in[r], fox_b_f[r], fox_w_o[r])
        h = h + y
        h = h + _conv_ffn(_rmsnorm(h, norm_ffn[i]), ffn_w_in[i], ffn_conv_w[i], ffn_conv_b[i], ffn_w_out[i])
    return _rmsnorm(h, final_norm)
```

```python
import functools
import math

import jax
import jax.numpy as jnp
from jax import lax
from jax.experimental import pallas as pl
from jax.experimental.pallas import tpu as pltpu

F32 = jnp.float32
BF16 = jnp.bfloat16
EPS = 1e-6
NEG = -1e30

LANES = 128
VMEM_LIMIT_BYTES = 56 * 1024 * 1024
BLK = 128

ROPE_THETA = 10000.0
REL_BUCKETS = 32
REL_MAX_DIST = 2048
MLA_HEADS, MLA_Q_RANK, MLA_KV_RANK, MLA_NOPE, MLA_ROPE, MLA_V = 16, 512, 512, 128, 64, 128
SWA_Q_HEADS, SWA_KV_HEADS, SWA_HEAD_DIM, SWA_WINDOW = 32, 4, 64, 128
DIL_HEADS, DIL_HEAD_DIM = 32, 64
DIL_PAIRS = ((128, 1), (512, 4), (2048, 16))
FOX_HEADS, FOX_HEAD_DIM = 16, 128
CONV_WIDTH = 3
CONV_HALO = 16


def _cparams(*sem):
    return pltpu.CompilerParams(dimension_semantics=sem, vmem_limit_bytes=VMEM_LIMIT_BYTES)


def _rms(x, g):
    return x * lax.rsqrt(jnp.mean(x * x, axis=-1, keepdims=True) + EPS) * g


def _dot(a, b):
    return jnp.dot(a, b, preferred_element_type=F32)


def _dot_nt(a, b):
    return lax.dot_general(a, b, (((1,), (1,)), ((), ())), preferred_element_type=F32)


def _rope_tab_kernel(pos_ref, f_ref, sgn_ref, c_ref, s_ref):
    ang = pos_ref[...].astype(F32) * f_ref[...]
    c_ref[...] = jnp.cos(ang)
    s_ref[...] = jnp.sin(ang) * sgn_ref[...]


def _rope_tables(positions):
    t = positions.shape[0]
    half = MLA_ROPE // 2
    inv_freq = ROPE_THETA ** (-jnp.arange(half, dtype=F32) / half)
    zeros = jnp.zeros((LANES - 2 * half,), F32)
    freq = jnp.concatenate([inv_freq, inv_freq, zeros])[None]
    sgn = jnp.concatenate([-jnp.ones((half,), F32), jnp.ones((half,), F32), zeros])[None]
    tm = min(t, 1024)
    row = pl.BlockSpec((tm, LANES), lambda i: (i, 0))
    vec = pl.BlockSpec((1, LANES), lambda i: (0, 0))
    return pl.pallas_call(
        _rope_tab_kernel,
        out_shape=(jax.ShapeDtypeStruct((t, LANES), F32),) * 2,
        grid=(t // tm,),
        in_specs=[pl.BlockSpec((tm, 1), lambda i: (i, 0)), vec, vec],
        out_specs=(row, row),
        compiler_params=_cparams("arbitrary"),
        name="rope_tables",
    )(positions.reshape(t, 1), freq, sgn)


def _rope128(z, c, s):
    half = MLA_ROPE // 2
    lane = lax.broadcasted_iota(jnp.int32, z.shape, 1)
    swapped = jnp.where(lane < half, pltpu.roll(z, LANES - half, 1), pltpu.roll(z, half, 1))
    return z * c + swapped * s


def _proj_kernel(*refs, norm, n_extra, epilogue):
    if norm:
        lhs_ref, g_ref, w_ref = refs[:3]
        rest = refs[3:]
        a_ref = rest[-1]
        rest = rest[:-1]

        @pl.when(pl.program_id(1) == 0)
        def _():
            a_ref[...] = _rms(lhs_ref[...], g_ref[...]).astype(BF16)

        a = a_ref[...]
    else:
        lhs_ref, w_ref = refs[:2]
        rest = refs[2:]
        a = lhs_ref[...]
    y = _dot(a, w_ref[...])
    epilogue(y, rest[:n_extra], rest[n_extra:])


def _proj(name, lhs, gain, w, extras, outs, epilogue, tm, tn):
    m, k = lhs.shape
    n = w.shape[1]
    norm = gain is not None
    in_specs = [pl.BlockSpec((tm, k), lambda i, j: (i, 0))]
    args = [lhs]
    if norm:
        in_specs.append(pl.BlockSpec((1, k), lambda i, j: (0, 0)))
        args.append(gain.reshape(1, k))
    in_specs.append(pl.BlockSpec((k, tn), lambda i, j: (0, j)))
    args.append(w)
    for arr, spec in extras:
        in_specs.append(spec)
        args.append(arr)
    res = pl.pallas_call(
        functools.partial(_proj_kernel, norm=norm, n_extra=len(extras), epilogue=epilogue),
        out_shape=tuple(jax.ShapeDtypeStruct(s, d) for s, d, _ in outs),
        grid=(m // tm, n // tn),
        in_specs=in_specs,
        out_specs=tuple(spec for _, _, spec in outs),
        scratch_shapes=[pltpu.VMEM((tm, k), BF16)] if norm else [],
        compiler_params=_cparams("arbitrary", "arbitrary"),
        name=name,
    )(*args)
    return res


def _out_proj_kernel(lhs_ref, w_ref, b_ref, res_ref, o_ref):
    o_ref[...] = res_ref[...] + (_dot(lhs_ref[...], w_ref[...]) + b_ref[...])


def _out_proj(name, lhs, w, bias, res, tm=512):
    m, k = lhs.shape
    n = w.shape[1]
    tm = min(tm, m)
    return pl.pallas_call(
        _out_proj_kernel,
        out_shape=jax.ShapeDtypeStruct((m, n), F32),
        grid=(m // tm,),
        in_specs=[
            pl.BlockSpec((tm, k), lambda i: (i, 0)),
            pl.BlockSpec((k, n), lambda i: (0, 0)),
            pl.BlockSpec((1, n), lambda i: (0, 0)),
            pl.BlockSpec((tm, n), lambda i: (i, 0)),
        ],
        out_specs=pl.BlockSpec((tm, n), lambda i: (i, 0)),
        compiler_params=_cparams("arbitrary"),
        name=name,
    )(lhs, w, bias, res)


def _flash_kernel(*refs, bq, dv, fox):
    if fox:
        q_ref, k_ref, v_ref, cum_ref, cumt_ref, o_ref, m_sc, l_sc, acc_sc = refs
    else:
        q_ref, k_ref, v_ref, o_ref, m_sc, l_sc, acc_sc = refs
    h = pl.program_id(0)
    i = pl.program_id(1)
    m_sc[...] = jnp.full(m_sc.shape, NEG, F32)
    l_sc[...] = jnp.zeros(l_sc.shape, F32)
    acc_sc[...] = jnp.zeros(acc_sc.shape, F32)
    q = q_ref[0]
    if fox:
        lane = lax.broadcasted_iota(jnp.int32, (bq, LANES), 1)
        cq = jnp.sum(jnp.where(lane == h, cum_ref[...], 0.0), axis=1, keepdims=True)

    def chunk(kb, masked):
        k0 = pl.multiple_of(kb * bq, bq)
        s = _dot_nt(q, k_ref[0, pl.ds(k0, bq), :])
        if fox:
            s = s + cq - cumt_ref[pl.ds(h, 1), pl.ds(k0, bq)]
        if masked:
            row = lax.broadcasted_iota(jnp.int32, (bq, bq), 0)
            col = lax.broadcasted_iota(jnp.int32, (bq, bq), 1)
            s = jnp.where(col <= row, s, NEG)
        m_prev = m_sc[...]
        m_new = jnp.maximum(m_prev, jnp.max(s, axis=1, keepdims=True))
        alpha = jnp.exp(m_prev - m_new)
        p = jnp.exp(s - m_new)
        l_sc[...] = alpha * l_sc[...] + jnp.sum(p, axis=1, keepdims=True)
        acc_sc[...] = alpha * acc_sc[...] + _dot(p.astype(BF16), v_ref[0, pl.ds(k0, bq), :])
        m_sc[...] = m_new

    def body(kb, carry):
        chunk(kb, False)
        return carry

    lax.fori_loop(0, i, body, 0)
    chunk(i, True)
    o_ref[...] = (acc_sc[...] / l_sc[...]).astype(BF16)


def _flash(name, q, k, v, q_off, k_off, v_off, heads, cum=None, cumt=None, bq=512):
    t = q.shape[1]
    dk = q.shape[2]
    dv = v.shape[2]
    bq = min(bq, t)
    fox = cum is not None
    in_specs = [
        pl.BlockSpec((1, bq, dk), lambda h, i: (q_off + h, i, 0)),
        pl.BlockSpec((1, t, dk), lambda h, i: (k_off + h, 0, 0)),
        pl.BlockSpec((1, t, dv), lambda h, i: (v_off + h, 0, 0)),
    ]
    args = [q, k, v]
    if fox:
        in_specs += [
            pl.BlockSpec((bq, LANES), lambda h, i: (i, 0)),
            pl.BlockSpec(cumt.shape, lambda h, i: (0, 0)),
        ]
        args += [cum, cumt]
    return pl.pallas_call(
        functools.partial(_flash_kernel, bq=bq, dv=dv, fox=fox),
        out_shape=jax.ShapeDtypeStruct((t, heads * dv), BF16),
        grid=(heads, t // bq),
        in_specs=in_specs,
        out_specs=pl.BlockSpec((bq, dv), lambda h, i: (i, h)),
        scratch_shapes=[pltpu.VMEM((bq, 1), F32), pltpu.VMEM((bq, 1), F32), pltpu.VMEM((bq, dv), F32)],
        compiler_params=_cparams("arbitrary", "arbitrary"),
        name=name,
    )(*args)


def _band_kernel(*refs, nbk, grp, sinks):
    if sinks:
        q_ref, k_ref, v_ref, kp_ref, vp_ref, tab_ref, sink_ref, o_ref, kf_sc, vf_sc = refs
    else:
        q_ref, k_ref, v_ref, kp_ref, vp_ref, tab_ref, o_ref, m_ref, l_ref, kf_sc, vf_sc = refs
    i = pl.program_id(0)
    j = pl.program_id(1)
    kf_sc[0:BLK, :] = kp_ref[...]
    kf_sc[BLK:, :] = k_ref[...]
    vf_sc[0:BLK, :] = vp_ref[...]
    vf_sc[BLK:, :] = v_ref[...]
    d = LANES // 2
    col = lax.broadcasted_iota(jnp.int32, (1, 2 * BLK), 1)
    if not sinks:
        stat_lane = lax.broadcasted_iota(jnp.int32, (BLK, LANES), 1)

        @pl.when(j % (LANES // 2) == 0)
        def _():
            m_ref[...] = jnp.zeros(m_ref.shape, F32)
            l_ref[...] = jnp.zeros(l_ref.shape, F32)

    def block(b, carry):
        r0 = pl.multiple_of(b * BLK, BLK)
        pen = jnp.where(jnp.logical_and(i == 0, b == 0), NEG, 0.0)
        penrow = jnp.where(col < BLK, pen, 0.0)
        outs = []
        for kh in range(2):
            kb = kf_sc[pl.ds(r0, 2 * BLK), kh * d:(kh + 1) * d]
            vb = vf_sc[pl.ds(r0, 2 * BLK), kh * d:(kh + 1) * d]
            for g in range(grp):
                hq = kh * grp + g
                q = q_ref[pl.ds(r0, BLK), hq * d:(hq + 1) * d]
                s = _dot_nt(q, kb) + tab_ref[hq] + penrow
                m = jnp.max(s, axis=1, keepdims=True)
                p = jnp.exp(s - m)
                l = jnp.sum(p, axis=1, keepdims=True)
                acc = _dot(p.astype(BF16), vb)
                if sinks:
                    sink = sink_ref[:, hq * d:(hq + 1) * d]
                    m2 = jnp.maximum(m, sink)
                    a = jnp.exp(m - m2)
                    den = l * a + jnp.exp(sink - m2)
                    outs.append(acc * (a / den))
                else:
                    outs.append(acc / l)
                    sel = stat_lane == (2 * j + kh) % LANES
                    m_ref[pl.ds(r0, BLK), :] = jnp.where(sel, m, m_ref[pl.ds(r0, BLK), :])
                    l_ref[pl.ds(r0, BLK), :] = jnp.where(sel, l, l_ref[pl.ds(r0, BLK), :])
        o_ref[pl.ds(r0, BLK), :] = jnp.concatenate(outs, axis=1).astype(BF16)
        return carry

    lax.fori_loop(0, nbk, block, 0)


def _band(name, x, width, q_blk, k_blk, v_blk, n_steps, grp, tab, tab_blk, sink_rep=None, tile=512):
    length = x.shape[0]
    tb = min(tile, length)
    nbk = tb // BLK
    qw = LANES * grp
    n_out = n_steps * qw
    sinks = sink_rep is not None
    prev = lambda i, j: jnp.maximum(i * nbk - 1, 0)
    in_specs = [
        pl.BlockSpec((tb, qw), lambda i, j: (i, q_blk(j))),
        pl.BlockSpec((tb, LANES), lambda i, j: (i, k_blk(j))),
        pl.BlockSpec((tb, LANES), lambda i, j: (i, v_blk(j))),
        pl.BlockSpec((BLK, LANES), lambda i, j: (prev(i, j), k_blk(j))),
        pl.BlockSpec((BLK, LANES), lambda i, j: (prev(i, j), v_blk(j))),
        pl.BlockSpec((2 * grp, BLK, 2 * BLK), lambda i, j: (tab_blk(j), 0, 0)),
    ]
    args = [x, x, x, x, x, tab]
    out_shape = [jax.ShapeDtypeStruct((length, n_out), BF16)]
    out_specs = [pl.BlockSpec((tb, qw), lambda i, j: (i, j))]
    if sinks:
        in_specs.append(pl.BlockSpec((1, qw), lambda i, j: (0, j)))
        args.append(sink_rep)
    else:
        n_stat = max(LANES, 2 * n_steps)
        stat = pl.BlockSpec((tb, LANES), lambda i, j: (i, j // (LANES // 2)))
        out_shape += [jax.ShapeDtypeStruct((length, n_stat), F32)] * 2
        out_specs += [stat, stat]
    return pl.pallas_call(
        functools.partial(_band_kernel, nbk=nbk, grp=grp, sinks=sinks),
        out_shape=tuple(out_shape),
        grid=(length // tb, n_steps),
        in_specs=in_specs,
        out_specs=tuple(out_specs),
        scratch_shapes=[pltpu.VMEM((tb + BLK, LANES), BF16), pltpu.VMEM((tb + BLK, LANES), BF16)],
        compiler_params=_cparams("arbitrary", "arbitrary"),
        name=name,
    )(*args)


def _t5_bucket(n):
    exact = REL_BUCKETS // 2
    nf = jnp.maximum(n, 1).astype(F32)
    large = exact + (jnp.log(nf / exact) / math.log(REL_MAX_DIST / exact) * (REL_BUCKETS - exact)).astype(jnp.int32)
    return jnp.where(n < exact, n, jnp.minimum(large, REL_BUCKETS - 1))


def _band_table(rel_bias, rate, max_dist):
    qi = jnp.arange(BLK)[:, None]
    kj = jnp.arange(2 * BLK)[None, :]
    dist = qi + BLK - kj
    valid = (dist >= 0) & (dist <= max_dist)
    row = rel_bias[_t5_bucket(rate * jnp.arange(max_dist + 1))].T
    tab = row[:, jnp.clip(dist, 0, max_dist)]
    return jnp.where(valid[None], tab, NEG).astype(F32)


def _dil_out_kernel(o1, o2, o3, m1, m2, m3, l1, l2, l3, e_ref, w_ref, res_ref, out_ref):
    ms = [m1[...], m2[...], m3[...]]
    ls = [l1[...], l2[...], l3[...]]
    os_ = [o1, o2, o3]
    lane = lax.broadcasted_iota(jnp.int32, ms[0].shape, 1)
    mx = jnp.maximum(jnp.maximum(ms[0], ms[1]), ms[2])
    ws = [l * jnp.exp(m - mx) for m, l in zip(ms, ls)]
    den = ws[0] + ws[1] + ws[2]
    e = e_ref[...]
    o = None
    for w, o_ref in zip(ws, os_):
        wn = jnp.where(lane < DIL_HEADS, w / den, 0.0)
        hi = wn.astype(BF16)
        lo = (wn - hi.astype(F32)).astype(BF16)
        term = o_ref[...].astype(F32) * (_dot(hi, e) + _dot(lo, e))
        o = term if o is None else o + term
    out_ref[...] = res_ref[...] + _dot(o.astype(BF16), w_ref[...])


def _dil_out(os_, ms, ls, expand, w, res, tm=512):
    t, n = res.shape
    tm = min(tm, t)
    row = pl.BlockSpec((tm, n), lambda i: (i, 0))
    stat = pl.BlockSpec((tm, LANES), lambda i: (i, 0))
    return pl.pallas_call(
        _dil_out_kernel,
        out_shape=jax.ShapeDtypeStruct((t, n), F32),
        grid=(t // tm,),
        in_specs=[row] * 3 + [stat] * 6 + [
            pl.BlockSpec((LANES, n), lambda i: (0, 0)),
            pl.BlockSpec((n, n), lambda i: (0, 0)),
            row,
        ],
        out_specs=row,
        compiler_params=_cparams("arbitrary"),
        name="dil_out",
    )(*os_, *ms, *ls, expand, w, res)


def _cumsum_kernel(x_ref, c_ref, ct_ref, carry_sc, *, tb):
    @pl.when(pl.program_id(0) == 0)
    def _():
        carry_sc[...] = jnp.zeros(carry_sc.shape, F32)

    x = x_ref[...]
    row = lax.broadcasted_iota(jnp.int32, (tb, tb), 0)
    colm = lax.broadcasted_iota(jnp.int32, (tb, tb), 1)
    tri = jnp.where(colm <= row, 1.0, 0.0).astype(BF16)
    hi = x.astype(BF16)
    r1 = x - hi.astype(F32)
    mid = r1.astype(BF16)
    lo = (r1 - mid.astype(F32)).astype(BF16)
    c = carry_sc[...] + (_dot(tri, hi) + _dot(tri, mid) + _dot(tri, lo))
    c_ref[...] = c
    ct_ref[...] = c.T
    carry_sc[...] = c[tb - 1:tb, :]


def _cumsum(x, tb=256):
    t = x.shape[0]
    tb = min(tb, t)
    return pl.pallas_call(
        functools.partial(_cumsum_kernel, tb=tb),
        out_shape=(jax.ShapeDtypeStruct((t, LANES), F32), jax.ShapeDtypeStruct((LANES, t), F32)),
        grid=(t // tb,),
        in_specs=[pl.BlockSpec((tb, LANES), lambda i: (i, 0))],
        out_specs=(pl.BlockSpec((tb, LANES), lambda i: (i, 0)), pl.BlockSpec((LANES, tb), lambda i: (0, i))),
        scratch_shapes=[pltpu.VMEM((1, LANES), F32)],
        compiler_params=_cparams("arbitrary"),
        name="fox_cumsum",
    )(x)


def _ffn_kernel(h_ref, hp_ref, g_ref, wg_ref, wv_ref, cwg_ref, cwv_ref, cbg_ref, cbv_ref, wo_ref, o_ref, a_sc, *, tm):
    i = pl.program_id(0)
    j = pl.program_id(1)

    @pl.when(j == 0)
    def _():
        g = g_ref[...]
        halo = _rms(hp_ref[...], g)
        a_sc[0:CONV_HALO, :] = jnp.where(i == 0, 0.0, halo).astype(BF16)
        a_sc[CONV_HALO:, :] = _rms(h_ref[...], g).astype(BF16)
        o_ref[...] = h_ref[...]

    a = a_sc[...]

    def conv(w_ref, cw_ref, cb_ref):
        u = _dot(a, w_ref[...])
        cw = cw_ref[...]
        c = cb_ref[...] + pltpu.roll(u, 2, 0)[CONV_HALO:] * cw[0:1]
        c = c + pltpu.roll(u, 1, 0)[CONV_HALO:] * cw[1:2]
        return c + u[CONV_HALO:] * cw[2:3]

    gate = conv(wg_ref, cwg_ref, cbg_ref)
    val = conv(wv_ref, cwv_ref, cbv_ref)
    act = (gate / (1.0 + jnp.exp(-gate))) * val
    o_ref[...] += _dot(act.astype(BF16), wo_ref[...])


def _ffn(h, gain, w_in, conv_w, conv_b, w_out, tm=512, tf=512):
    t, dm = h.shape
    ff = w_out.shape[0]
    tm = min(tm, t)
    nf = ff // tf
    halo_blocks = tm // CONV_HALO
    return pl.pallas_call(
        functools.partial(_ffn_kernel, tm=tm),
        out_shape=jax.ShapeDtypeStruct((t, dm), F32),
        grid=(t // tm, nf),
        in_specs=[
            pl.BlockSpec((tm, dm), lambda i, j: (i, 0)),
            pl.BlockSpec((CONV_HALO, dm), lambda i, j: (jnp.maximum(i * halo_blocks - 1, 0), 0)),
            pl.BlockSpec((1, dm), lambda i, j: (0, 0)),
            pl.BlockSpec((dm, tf), lambda i, j: (0, j)),
            pl.BlockSpec((dm, tf), lambda i, j: (0, j + nf)),
            pl.BlockSpec((CONV_WIDTH, tf), lambda i, j: (0, j)),
            pl.BlockSpec((CONV_WIDTH, tf), lambda i, j: (0, j + nf)),
            pl.BlockSpec((1, tf), lambda i, j: (0, j)),
            pl.BlockSpec((1, tf), lambda i, j: (0, j + nf)),
            pl.BlockSpec((tf, dm), lambda i, j: (j, 0)),
        ],
        out_specs=pl.BlockSpec((tm, dm), lambda i, j: (i, 0)),
        scratch_shapes=[pltpu.VMEM((tm + CONV_HALO, dm), BF16)],
        compiler_params=_cparams("arbitrary", "arbitrary"),
        name="conv_ffn",
    )(h, h, gain.reshape(1, dm), w_in, w_in, conv_w, conv_w, conv_b.reshape(1, -1), conv_b.reshape(1, -1), w_out)


def _norm_kernel(x_ref, g_ref, o_ref):
    o_ref[...] = _rms(x_ref[...], g_ref[...])


def _final_norm(h, gain, tm=512):
    t, dm = h.shape
    tm = min(tm, t)
    row = pl.BlockSpec((tm, dm), lambda i: (i, 0))
    return pl.pallas_call(
        _norm_kernel,
        out_shape=jax.ShapeDtypeStruct((t, dm), F32),
        grid=(t // tm,),
        in_specs=[row, pl.BlockSpec((1, dm), lambda i: (0, 0))],
        out_specs=row,
        compiler_params=_cparams("arbitrary"),
        name="final_norm",
    )(h, gain.reshape(1, dm))


def _mla(h, gain, positions, w_in, g_q, g_kv, w_qb, w_kvb, w_o, tm=512):
    t, dm = h.shape
    tm = min(tm, t)
    hh = MLA_HEADS
    qk = MLA_NOPE + MLA_ROPE
    lat_w = MLA_Q_RANK + MLA_KV_RANK + LANES
    cos_t, sin_t = _rope_tables(positions)
    w_in_p = jnp.pad(w_in, ((0, 0), (0, lat_w - w_in.shape[1]))).astype(BF16)
    w_qb_p = jnp.pad(w_qb.reshape(MLA_Q_RANK, hh, qk), ((0, 0), (0, 0), (0, 2 * LANES - qk)))
    w_qb_p = w_qb_p.reshape(MLA_Q_RANK, hh * 2 * LANES).astype(BF16)

    row128 = pl.BlockSpec((tm, LANES), lambda i, j: (i, 0))

    def lat_epilogue(y, extras, outs):
        gq_ref, gkv_ref, c_ref, s_ref = extras
        cq_ref, ckv_ref, kr_ref = outs
        cq_ref[...] = _rms(y[:, :MLA_Q_RANK], gq_ref[...]).astype(BF16)
        ckv_ref[...] = _rms(y[:, MLA_Q_RANK:MLA_Q_RANK + MLA_KV_RANK], gkv_ref[...]).astype(BF16)
        kr_ref[...] = _rope128(y[:, MLA_Q_RANK + MLA_KV_RANK:], c_ref[...], s_ref[...]).astype(BF16)

    rank_row = pl.BlockSpec((tm, MLA_Q_RANK), lambda i, j: (i, 0))
    c_q, c_kv, k_rope = _proj(
        "mla_latents", h, gain, w_in_p,
        extras=[(g_q.reshape(1, -1), pl.BlockSpec((1, MLA_Q_RANK), lambda i, j: (0, 0))),
                (g_kv.reshape(1, -1), pl.BlockSpec((1, MLA_KV_RANK), lambda i, j: (0, 0))),
                (cos_t, row128), (sin_t, row128)],
        outs=[((t, MLA_Q_RANK), BF16, rank_row), ((t, MLA_KV_RANK), BF16, rank_row), ((t, LANES), BF16, row128)],
        epilogue=lat_epilogue, tm=tm, tn=lat_w)

    heads_per_tile = 2
    tn = heads_per_tile * 2 * LANES
    scale = qk ** -0.5

    def q_epilogue(y, extras, outs):
        c_ref, s_ref = extras
        (q_ref,) = outs
        y = y * scale
        for a in range(heads_per_tile):
            base = a * 2 * LANES
            q_ref[a, :, :LANES] = y[:, base:base + LANES].astype(BF16)
            q_ref[a, :, LANES:] = _rope128(y[:, base + LANES:base + 2 * LANES], c_ref[...], s_ref[...]).astype(BF16)

    head_blk = pl.BlockSpec((heads_per_tile, tm, 2 * LANES), lambda i, j: (j, i, 0))
    (q_cat,) = _proj(
        "mla_q", c_q, None, w_qb_p,
        extras=[(cos_t, row128), (sin_t, row128)],
        outs=[((hh, t, 2 * LANES), BF16, head_blk)],
        epilogue=q_epilogue, tm=tm, tn=tn)

    def kv_epilogue(y, extras, outs):
        (kr_ref,) = extras
        k_ref, v_ref = outs
        for a in range(heads_per_tile):
            base = a * 2 * LANES
            k_ref[a, :, :LANES] = y[:, base:base + LANES].astype(BF16)
            k_ref[a, :, LANES:] = kr_ref[...]
            v_ref[a] = y[:, base + LANES:base + 2 * LANES].astype(BF16)

    k_cat, v = _proj(
        "mla_kv", c_kv, None, w_kvb.astype(BF16),
        extras=[(k_rope, row128)],
        outs=[((hh, t, 2 * LANES), BF16, head_blk),
              ((hh, t, LANES), BF16, pl.BlockSpec((heads_per_tile, tm, LANES), lambda i, j: (j, i, 0)))],
        epilogue=kv_epilogue, tm=tm, tn=tn)

    o = _flash("mla_attn", q_cat, k_cat, v, 0, 0, 0, hh)
    return _out_proj("mla_out", o, w_o.astype(BF16), jnp.zeros((1, dm), F32), h)


def _cols_epilogue(y, extras, outs):
    b_ref, sc_ref = extras
    (o_ref,) = outs
    o_ref[...] = ((y + b_ref[...]) * sc_ref[...]).astype(BF16)


def _swa(h, gain, rel_bias, w_qkv, b_qkv, sinks, w_o, b_o, tm=512, tn=512):
    t, dm = h.shape
    tm = min(tm, t)
    n = w_qkv.shape[1]
    nq = SWA_Q_HEADS * SWA_HEAD_DIM
    grp = SWA_Q_HEADS // SWA_KV_HEADS
    colscale = jnp.concatenate([jnp.full((nq,), SWA_HEAD_DIM ** -0.5, F32), jnp.ones((n - nq,), F32)])[None]
    vec = pl.BlockSpec((1, tn), lambda i, j: (0, j))
    (qkv,) = _proj(
        "swa_qkv", h, gain, w_qkv.astype(BF16),
        extras=[(b_qkv.reshape(1, n), vec), (colscale, vec)],
        outs=[((t, n), BF16, pl.BlockSpec((tm, tn), lambda i, j: (i, j)))],
        epilogue=_cols_epilogue, tm=tm, tn=tn)
    tab = _band_table(rel_bias, 1, SWA_WINDOW - 1)
    sink_rep = jnp.repeat(sinks.astype(F32), SWA_HEAD_DIM)[None]
    k0 = nq // LANES
    v0 = k0 + SWA_KV_HEADS * SWA_HEAD_DIM // LANES
    (o,) = _band("swa_attn", qkv, n, lambda j: j, lambda j: k0 + j, lambda j: v0 + j, SWA_KV_HEADS // 2, grp,
                 tab, lambda j: j, sink_rep=sink_rep, tile=256)
    return _out_proj("swa_out", o, w_o.astype(BF16), b_o.reshape(1, dm), h)


def _dilated(h, gain, rel_bias, w_qkv, w_o, tm=512, tn=512):
    t, dm = h.shape
    tm = min(tm, t)
    n = w_qkv.shape[1]
    nq = DIL_HEADS * DIL_HEAD_DIM
    colscale = jnp.concatenate([jnp.full((nq,), DIL_HEAD_DIM ** -0.5, F32), jnp.ones((n - nq,), F32)])[None]
    vec = pl.BlockSpec((1, tn), lambda i, j: (0, j))
    (qkv,) = _proj(
        "dil_qkv", h, gain, w_qkv.astype(BF16),
        extras=[(jnp.zeros((1, n), F32), vec), (colscale, vec)],
        outs=[((t, n), BF16, pl.BlockSpec((tm, tn), lambda i, j: (i, j)))],
        epilogue=_cols_epilogue, tm=tm, tn=tn)
    pairs = DIL_HEADS // 2
    cols = n // LANES
    os_, ms, ls = [], [], []
    for window, rate in DIL_PAIRS:
        n_keys = window // rate
        tab = _band_table(rel_bias, rate, n_keys)
        x = qkv.reshape(t // rate, rate * n)
        o, m, l = _band(
            f"dil_attn_r{rate}", x, n,
            lambda j: (j // pairs) * cols + j % pairs,
            lambda j: (j // pairs) * cols + pairs + j % pairs,
            lambda j: (j // pairs) * cols + 2 * pairs + j % pairs,
            rate * pairs, 1, tab, lambda j: j % pairs)
        os_.append(o.reshape(t, nq))
        ms.append(m[:, :rate * DIL_HEADS].reshape(t, DIL_HEADS))
        ls.append(l[:, :rate * DIL_HEADS].reshape(t, DIL_HEADS))
    pad = lambda a: jnp.pad(a, ((0, 0), (0, LANES - DIL_HEADS)))
    ms = [pad(m) for m in ms]
    ls = [pad(l) for l in ls]
    expand = jnp.repeat(jnp.eye(LANES, DIL_HEADS, dtype=BF16), DIL_HEAD_DIM, axis=1)
    return _dil_out(os_, ms, ls, expand, w_o.astype(BF16), h)


def _fox(h, gain, w_in, b_f, w_o, tm=512, tn=512):
    t, dm = h.shape
    tm = min(tm, t)
    hh = FOX_HEADS
    hd = hh * FOX_HEAD_DIM
    heads_per_tile = tn // FOX_HEAD_DIM
    colscale = jnp.concatenate([jnp.full((hd,), FOX_HEAD_DIM ** -0.5, F32), jnp.ones((2 * hd,), F32)])[None]

    def qkv_epilogue(y, extras, outs):
        (sc_ref,) = extras
        (o_ref,) = outs
        y = y * sc_ref[...]
        for a in range(heads_per_tile):
            o_ref[a] = y[:, a * FOX_HEAD_DIM:(a + 1) * FOX_HEAD_DIM].astype(BF16)

    (qkv,) = _proj(
        "fox_qkv", h, gain, w_in[:, :3 * hd].astype(BF16),
        extras=[(colscale, pl.BlockSpec((1, tn), lambda i, j: (0, j)))],
        outs=[((3 * hh, t, FOX_HEAD_DIM), BF16,
               pl.BlockSpec((heads_per_tile, tm, FOX_HEAD_DIM), lambda i, j: (j, i, 0)))],
        epilogue=qkv_epilogue, tm=tm, tn=tn)

    def gate_epilogue(y, extras, outs):
        (b_ref,) = extras
        (o_ref,) = outs
        x = y + b_ref[...]
        o_ref[...] = jnp.minimum(x, 0.0) - jnp.log(1.0 + jnp.exp(-jnp.abs(x)))

    w_gate = jnp.pad(w_in[:, 3 * hd:], ((0, 0), (0, LANES - hh))).astype(BF16)
    b_gate = jnp.pad(b_f.astype(F32), (0, LANES - hh))[None]
    (log_f,) = _proj(
        "fox_gate", h, gain, w_gate,
        extras=[(b_gate, pl.BlockSpec((1, LANES), lambda i, j: (0, 0)))],
        outs=[((t, LANES), F32, pl.BlockSpec((tm, LANES), lambda i, j: (i, 0)))],
        epilogue=gate_epilogue, tm=tm, tn=LANES)
    cum, cum_t = _cumsum(log_f)
    o = _flash("fox_attn", qkv, qkv, qkv, 0, hh, 2 * hh, hh, cum=cum, cumt=cum_t)
    return _out_proj("fox_out", o, w_o.astype(BF16), jnp.zeros((1, dm), F32), h)


def kernel(x, positions, rel_bias, norm_mix, norm_ffn, mla_w_in, mla_g_q, mla_g_kv, mla_w_qb, mla_w_kvb, mla_w_o, swa_w_qkv, swa_b_qkv, swa_sinks, swa_w_o, swa_b_o, dil_w_qkv, dil_w_o, fox_w_in, fox_b_f, fox_w_o, ffn_w_in, ffn_conv_w, ffn_conv_b, ffn_w_out, final_norm):
    bsz, t, dm = x.shape
    assert bsz == 1
    depth = norm_mix.shape[0]
    h = x.reshape(t, dm)
    pos = positions.reshape(t)
    for i in range(depth):
        kind = i % 4
        r = i // 4
        if kind == 0:
            h = _mla(h, norm_mix[i], pos, mla_w_in[r], mla_g_q[r], mla_g_kv[r], mla_w_qb[r], mla_w_kvb[r], mla_w_o[r])
        elif kind == 1:
            h = _swa(h, norm_mix[i], rel_bias, swa_w_qkv[r], swa_b_qkv[r], swa_sinks[r], swa_w_o[r], swa_b_o[r])
        elif kind == 2:
            h = _dilated(h, norm_mix[i], rel_bias, dil_w_qkv[r], dil_w_o[r])
        else:
            h = _fox(h, norm_mix[i], fox_w_in[r], fox_b_f[r], fox_w_o[r])
        h = _ffn(h, norm_ffn[i], ffn_w_in[i].astype(BF16), ffn_conv_w[i], ffn_conv_b[i], ffn_w_out[i].astype(BF16))
    return _final_norm(h, final_norm).reshape(bsz, t, dm)
```

```python
import functools
import math

import jax
import jax.numpy as jnp
from jax import lax
from jax.experimental import pallas as pl
from jax.experimental.pallas import tpu as pltpu

F32 = jnp.float32
BF16 = jnp.bfloat16
EPS = 1e-6
NEG = -1e30
LOG2E = math.log2(math.e)

LANES = 128
VMEM_LIMIT_BYTES = 56 * 1024 * 1024
BLK = 128

ROPE_THETA = 10000.0
REL_BUCKETS = 32
REL_MAX_DIST = 2048
MLA_HEADS, MLA_Q_RANK, MLA_KV_RANK, MLA_NOPE, MLA_ROPE, MLA_V = 16, 512, 512, 128, 64, 128
SWA_Q_HEADS, SWA_KV_HEADS, SWA_HEAD_DIM, SWA_WINDOW = 32, 4, 64, 128
DIL_HEADS, DIL_HEAD_DIM = 32, 64
DIL_PAIRS = ((128, 1), (512, 4), (2048, 16))
FOX_HEADS, FOX_HEAD_DIM = 16, 128
CONV_WIDTH = 3
CONV_HALO = 16


def _cparams(*sem):
    return pltpu.CompilerParams(dimension_semantics=sem, vmem_limit_bytes=VMEM_LIMIT_BYTES)


def _rms(x, g):
    return x * lax.rsqrt(jnp.mean(x * x, axis=-1, keepdims=True) + EPS) * g


def _dot(a, b):
    return jnp.dot(a, b, preferred_element_type=F32)


def _dot_nt(a, b):
    return lax.dot_general(a, b, (((1,), (1,)), ((), ())), preferred_element_type=F32)


def _rope_tab_kernel(pos_ref, f_ref, sgn_ref, c_ref, s_ref):
    ang = pos_ref[...].astype(F32) * f_ref[...]
    c_ref[...] = jnp.cos(ang)
    s_ref[...] = jnp.sin(ang) * sgn_ref[...]


def _rope_tables(positions):
    t = positions.shape[0]
    half = MLA_ROPE // 2
    inv_freq = ROPE_THETA ** (-jnp.arange(half, dtype=F32) / half)
    zeros = jnp.zeros((LANES - 2 * half,), F32)
    freq = jnp.concatenate([inv_freq, inv_freq, zeros])[None]
    sgn = jnp.concatenate([-jnp.ones((half,), F32), jnp.ones((half,), F32), zeros])[None]
    tm = min(t, 1024)
    row = pl.BlockSpec((tm, LANES), lambda i: (i, 0))
    vec = pl.BlockSpec((1, LANES), lambda i: (0, 0))
    return pl.pallas_call(
        _rope_tab_kernel,
        out_shape=(jax.ShapeDtypeStruct((t, LANES), F32),) * 2,
        grid=(t // tm,),
        in_specs=[pl.BlockSpec((tm, 1), lambda i: (i, 0)), vec, vec],
        out_specs=(row, row),
        compiler_params=_cparams("arbitrary"),
        name="rope_tables",
    )(positions.reshape(t, 1), freq, sgn)


def _rope128(z, c, s):
    half = MLA_ROPE // 2
    lane = lax.broadcasted_iota(jnp.int32, z.shape, 1)
    swapped = jnp.where(lane < half, pltpu.roll(z, LANES - half, 1), pltpu.roll(z, half, 1))
    return z * c + swapped * s


def _proj_kernel(*refs, norm, n_extra, epilogue):
    if norm:
        lhs_ref, g_ref, w_ref = refs[:3]
        rest = refs[3:]
        a_ref = rest[-1]
        rest = rest[:-1]

        @pl.when(pl.program_id(1) == 0)
        def _():
            a_ref[...] = _rms(lhs_ref[...], g_ref[...]).astype(BF16)

        a = a_ref[...]
    else:
        lhs_ref, w_ref = refs[:2]
        rest = refs[2:]
        a = lhs_ref[...]
    y = _dot(a, w_ref[...])
    epilogue(y, rest[:n_extra], rest[n_extra:])


def _proj(name, lhs, gain, w, extras, outs, epilogue, tm, tn):
    m, k = lhs.shape
    n = w.shape[1]
    norm = gain is not None
    in_specs = [pl.BlockSpec((tm, k), lambda i, j: (i, 0))]
    args = [lhs]
    if norm:
        in_specs.append(pl.BlockSpec((1, k), lambda i, j: (0, 0)))
        args.append(gain.reshape(1, k))
    in_specs.append(pl.BlockSpec((k, tn), lambda i, j: (0, j)))
    args.append(w)
    for arr, spec in extras:
        in_specs.append(spec)
        args.append(arr)
    res = pl.pallas_call(
        functools.partial(_proj_kernel, norm=norm, n_extra=len(extras), epilogue=epilogue),
        out_shape=tuple(jax.ShapeDtypeStruct(s, d) for s, d, _ in outs),
        grid=(m // tm, n // tn),
        in_specs=in_specs,
        out_specs=tuple(spec for _, _, spec in outs),
        scratch_shapes=[pltpu.VMEM((tm, k), BF16)] if norm else [],
        compiler_params=_cparams("arbitrary", "arbitrary"),
        name=name,
    )(*args)
    return res


def _out_proj_kernel(lhs_ref, w_ref, b_ref, res_ref, o_ref):
    o_ref[...] = res_ref[...] + (_dot(lhs_ref[...], w_ref[...]) + b_ref[...])


def _out_proj(name, lhs, w, bias, res, tm=512):
    m, k = lhs.shape
    n = w.shape[1]
    tm = min(tm, m)
    return pl.pallas_call(
        _out_proj_kernel,
        out_shape=jax.ShapeDtypeStruct((m, n), F32),
        grid=(m // tm,),
        in_specs=[
            pl.BlockSpec((tm, k), lambda i: (i, 0)),
            pl.BlockSpec((k, n), lambda i: (0, 0)),
            pl.BlockSpec((1, n), lambda i: (0, 0)),
            pl.BlockSpec((tm, n), lambda i: (i, 0)),
        ],
        out_specs=pl.BlockSpec((tm, n), lambda i: (i, 0)),
        compiler_params=_cparams("arbitrary"),
        name=name,
    )(lhs, w, bias, res)


def _flash_kernel(*refs, bq, fox):
    if fox:
        q_ref, k_ref, v_ref, cum_ref, cumt_ref, o_ref, m_sc, acc_sc = refs
    else:
        q_ref, k_ref, v_ref, o_ref, m_sc, acc_sc = refs
    h = pl.program_id(0)
    i = pl.program_id(1)
    reps = bq // LANES
    m_sc[...] = jnp.full(m_sc.shape, NEG, F32)
    acc_sc[...] = jnp.zeros(acc_sc.shape, F32)
    q = q_ref[0]
    if fox:
        lane = lax.broadcasted_iota(jnp.int32, (bq, LANES), 1)
        cq = jnp.sum(jnp.where(lane == h, cum_ref[...], 0.0), axis=1, keepdims=True)
        cq = jnp.broadcast_to(cq, (bq, LANES))

    def scores(c):
        k0 = pl.multiple_of(c * bq, bq)
        return _dot_nt(q, k_ref[0, pl.ds(k0, bq), :])

    def softmax_pv(c, t, masked):
        k0 = pl.multiple_of(c * bq, bq)
        if fox:
            t = t - cumt_ref[pl.ds(h, 1), pl.ds(k0, bq)]
        if masked:
            row = lax.broadcasted_iota(jnp.int32, (bq, bq), 0)
            col = lax.broadcasted_iota(jnp.int32, (bq, bq), 1)
            t = jnp.where(col <= row, t, NEG)
        r = jnp.max(t, axis=1, keepdims=True)
        m_prev = m_sc[...]
        m_new = jnp.maximum(m_prev, r + cq) if fox else jnp.maximum(m_prev, r)
        alpha = jnp.exp2(m_prev - m_new)
        shift = m_new - cq if fox else m_new
        p = jnp.exp2(t - jnp.tile(shift, (1, reps)))
        pv = _dot(p.astype(BF16), v_ref[0, pl.ds(k0, bq), :])
        acc_sc[...] = jnp.tile(alpha, (1, 2)) * acc_sc[...] + pv
        m_sc[...] = m_new

    def body(c, s_cur):
        s_next = scores(c + 1)
        softmax_pv(c, s_cur, False)
        return s_next

    s_last = lax.fori_loop(0, i, body, scores(0))
    softmax_pv(i, s_last, True)
    acc = acc_sc[...]
    o_ref[...] = (acc[:, :LANES] / acc[:, LANES:]).astype(BF16)


def _flash(name, q, k, v, q_off, k_off, v_off, heads, cum=None, cumt=None, bq=512):
    t = q.shape[1]
    dk = q.shape[2]
    dv = v.shape[2]
    assert dv == 2 * LANES
    bq = min(bq, t)
    fox = cum is not None
    in_specs = [
        pl.BlockSpec((1, bq, dk), lambda h, i: (q_off + h, i, 0)),
        pl.BlockSpec((1, t, dk), lambda h, i: (k_off + h, 0, 0)),
        pl.BlockSpec((1, t, dv), lambda h, i: (v_off + h, 0, 0)),
    ]
    args = [q, k, v]
    if fox:
        in_specs += [
            pl.BlockSpec((bq, LANES), lambda h, i: (i, 0)),
            pl.BlockSpec(cumt.shape, lambda h, i: (0, 0)),
        ]
        args += [cum, cumt]
    return pl.pallas_call(
        functools.partial(_flash_kernel, bq=bq, fox=fox),
        out_shape=jax.ShapeDtypeStruct((t, heads * LANES), BF16),
        grid=(heads, t // bq),
        in_specs=in_specs,
        out_specs=pl.BlockSpec((bq, LANES), lambda h, i: (i, h)),
        scratch_shapes=[pltpu.VMEM((bq, LANES), F32), pltpu.VMEM((bq, dv), F32)],
        compiler_params=_cparams("arbitrary", "arbitrary"),
        name=name,
    )(*args)


def _band_kernel(*refs, nbk, npairs, grp, sinks):
    if sinks:
        q_ref, k_ref, v_ref, kp_ref, vp_ref, tab_ref, sink_ref, o_ref, kf_sc, vf_sc = refs
    else:
        q_ref, k_ref, v_ref, kp_ref, vp_ref, tab_ref, o_ref, m_ref, l_ref, kf_sc, vf_sc = refs
    i = pl.program_id(0)
    j = pl.program_id(1)
    kf_sc[0:BLK, :] = kp_ref[...]
    kf_sc[BLK:, :] = k_ref[...]
    vf_sc[0:BLK, :] = vp_ref[...]
    vf_sc[BLK:, :] = v_ref[...]
    d = LANES // 2
    n_kv = 2 * npairs
    col = lax.broadcasted_iota(jnp.int32, (1, 2 * BLK), 1)
    penrow = jnp.where(jnp.logical_and(col < BLK, i == 0), NEG, 0.0)
    if not sinks:
        stat_lane = lax.broadcasted_iota(jnp.int32, (BLK, LANES), 1)
        steps_per_stat = LANES // n_kv

        @pl.when(j % steps_per_stat == 0)
        def _():
            m_ref[...] = jnp.zeros(m_ref.shape, F32)
            l_ref[...] = jnp.zeros(l_ref.shape, F32)

    for b in range(nbk):
        r0 = b * BLK
        outs = []
        if not sinks:
            m_st = m_ref[r0:r0 + BLK, :]
            l_st = l_ref[r0:r0 + BLK, :]
        for kh in range(n_kv):
            kb = kf_sc[r0:r0 + 2 * BLK, kh * d:(kh + 1) * d]
            vb = vf_sc[r0:r0 + 2 * BLK, kh * d:(kh + 1) * d]
            for g in range(grp):
                hq = kh * grp + g
                q = q_ref[r0:r0 + BLK, hq * d:(hq + 1) * d]
                s = _dot_nt(q, kb) + tab_ref[hq]
                if b == 0:
                    s = s + penrow
                m = jnp.max(s, axis=1, keepdims=True)
                p = jnp.exp2(s - m)
                l = jnp.sum(p, axis=1, keepdims=True)
                acc = _dot(p.astype(BF16), vb)
                if sinks:
                    sink = sink_ref[:, hq * d:(hq + 1) * d]
                    m2 = jnp.maximum(m, sink)
                    a = jnp.exp2(m - m2)
                    den = l * a + jnp.exp2(sink - m2)
                    outs.append(acc * (a / den))
                else:
                    outs.append(acc / l)
                    sel = stat_lane == (n_kv * j + kh) % LANES
                    m_st = jnp.where(sel, m, m_st)
                    l_st = jnp.where(sel, l, l_st)
        o_ref[r0:r0 + BLK, :] = jnp.concatenate(outs, axis=1).astype(BF16)
        if not sinks:
            m_ref[r0:r0 + BLK, :] = m_st
            l_ref[r0:r0 + BLK, :] = l_st


def _band(name, x, q_blk, k_blk, v_blk, n_steps, npairs, grp, tab, tab_blk, sink_rep=None, tile=512):
    length = x.shape[0]
    tb = min(tile, length)
    nbk = tb // BLK
    kw = LANES * npairs
    qw = kw * grp
    n_kv = 2 * npairs
    n_out = n_steps * qw
    sinks = sink_rep is not None
    prev = lambda i, j: jnp.maximum(i * nbk - 1, 0)
    in_specs = [
        pl.BlockSpec((tb, qw), lambda i, j: (i, q_blk(j))),
        pl.BlockSpec((tb, kw), lambda i, j: (i, k_blk(j))),
        pl.BlockSpec((tb, kw), lambda i, j: (i, v_blk(j))),
        pl.BlockSpec((BLK, kw), lambda i, j: (prev(i, j), k_blk(j))),
        pl.BlockSpec((BLK, kw), lambda i, j: (prev(i, j), v_blk(j))),
        pl.BlockSpec((n_kv * grp, BLK, 2 * BLK), lambda i, j: (tab_blk(j), 0, 0)),
    ]
    args = [x, x, x, x, x, tab]
    out_shape = [jax.ShapeDtypeStruct((length, n_out), BF16)]
    out_specs = [pl.BlockSpec((tb, qw), lambda i, j: (i, j))]
    if sinks:
        in_specs.append(pl.BlockSpec((1, qw), lambda i, j: (0, j)))
        args.append(sink_rep)
    else:
        n_stat = max(LANES, n_kv * n_steps)
        stat = pl.BlockSpec((tb, LANES), lambda i, j: (i, j // (LANES // n_kv)))
        out_shape += [jax.ShapeDtypeStruct((length, n_stat), F32)] * 2
        out_specs += [stat, stat]
    return pl.pallas_call(
        functools.partial(_band_kernel, nbk=nbk, npairs=npairs, grp=grp, sinks=sinks),
        out_shape=tuple(out_shape),
        grid=(length // tb, n_steps),
        in_specs=in_specs,
        out_specs=tuple(out_specs),
        scratch_shapes=[pltpu.VMEM((tb + BLK, kw), BF16), pltpu.VMEM((tb + BLK, kw), BF16)],
        compiler_params=_cparams("arbitrary", "arbitrary"),
        name=name,
    )(*args)


def _t5_bucket(n):
    exact = REL_BUCKETS // 2
    nf = jnp.maximum(n, 1).astype(F32)
    large = exact + (jnp.log(nf / exact) / math.log(REL_MAX_DIST / exact) * (REL_BUCKETS - exact)).astype(jnp.int32)
    return jnp.where(n < exact, n, jnp.minimum(large, REL_BUCKETS - 1))


def _band_table(rel_bias, rate, max_dist):
    nh = rel_bias.shape[1]
    w = 2 * BLK
    dist = jnp.arange(w)
    row = rel_bias[_t5_bucket(rate * jnp.minimum(dist, max_dist))].T * LOG2E
    f = jnp.where(dist[None] <= max_dist, row, NEG).astype(F32)
    r = jnp.roll(f[:, ::-1], BLK + 1, axis=1)
    rr = jnp.concatenate([r, r], axis=1)
    flat = jnp.broadcast_to(rr[:, None, :], (nh, BLK + 1, 2 * w)).reshape(nh, (BLK + 1) * 2 * w)
    return flat[:, w:w + BLK * (2 * w - 1)].reshape(nh, BLK, 2 * w - 1)[:, :, :w]


def _dil_out_kernel(o1, o2, o3, m1, m2, m3, l1, l2, l3, e_ref, w_ref, res_ref, out_ref):
    ms = [m1[...], m2[...], m3[...]]
    ls = [l1[...], l2[...], l3[...]]
    os_ = [o1, o2, o3]
    lane = lax.broadcasted_iota(jnp.int32, ms[0].shape, 1)
    mx = jnp.maximum(jnp.maximum(ms[0], ms[1]), ms[2])
    ws = [l * jnp.exp2(m - mx) for m, l in zip(ms, ls)]
    den = ws[0] + ws[1] + ws[2]
    e = e_ref[...]
    o = None
    for w, o_ref in zip(ws, os_):
        wn = jnp.where(lane < DIL_HEADS, w / den, 0.0)
        hi = wn.astype(BF16)
        lo = (wn - hi.astype(F32)).astype(BF16)
        term = o_ref[...].astype(F32) * (_dot(hi, e) + _dot(lo, e))
        o = term if o is None else o + term
    out_ref[...] = res_ref[...] + _dot(o.astype(BF16), w_ref[...])


def _dil_out(os_, ms, ls, expand, w, res, tm=512):
    t, n = res.shape
    tm = min(tm, t)
    row = pl.BlockSpec((tm, n), lambda i: (i, 0))
    stat = pl.BlockSpec((tm, LANES), lambda i: (i, 0))
    return pl.pallas_call(
        _dil_out_kernel,
        out_shape=jax.ShapeDtypeStruct((t, n), F32),
        grid=(t // tm,),
        in_specs=[row] * 3 + [stat] * 6 + [
            pl.BlockSpec((LANES, n), lambda i: (0, 0)),
            pl.BlockSpec((n, n), lambda i: (0, 0)),
            row,
        ],
        out_specs=row,
        compiler_params=_cparams("arbitrary"),
        name="dil_out",
    )(*os_, *ms, *ls, expand, w, res)


def _cumsum_kernel(x_ref, c_ref, ct_ref, carry_sc, *, tb):
    @pl.when(pl.program_id(0) == 0)
    def _():
        carry_sc[...] = jnp.zeros(carry_sc.shape, F32)

    x = x_ref[...]
    row = lax.broadcasted_iota(jnp.int32, (tb, tb), 0)
    colm = lax.broadcasted_iota(jnp.int32, (tb, tb), 1)
    tri = jnp.where(colm <= row, 1.0, 0.0).astype(BF16)
    hi = x.astype(BF16)
    r1 = x - hi.astype(F32)
    mid = r1.astype(BF16)
    lo = (r1 - mid.astype(F32)).astype(BF16)
    c = carry_sc[...] + (_dot(tri, hi) + _dot(tri, mid) + _dot(tri, lo))
    c2 = c * LOG2E
    c_ref[...] = c2
    ct_ref[...] = c2.T
    carry_sc[...] = c[tb - 1:tb, :]


def _cumsum(x, tb=256):
    t = x.shape[0]
    tb = min(tb, t)
    return pl.pallas_call(
        functools.partial(_cumsum_kernel, tb=tb),
        out_shape=(jax.ShapeDtypeStruct((t, LANES), F32), jax.ShapeDtypeStruct((LANES, t), F32)),
        grid=(t // tb,),
        in_specs=[pl.BlockSpec((tb, LANES), lambda i: (i, 0))],
        out_specs=(pl.BlockSpec((tb, LANES), lambda i: (i, 0)), pl.BlockSpec((LANES, tb), lambda i: (0, i))),
        scratch_shapes=[pltpu.VMEM((1, LANES), F32)],
        compiler_params=_cparams("arbitrary"),
        name="fox_cumsum",
    )(x)


def _ffn_kernel(h_ref, hp_ref, g_ref, wg_ref, wv_ref, cwg_ref, cwv_ref, cbg_ref, cbv_ref, wo_ref, o_ref, a_sc, *, tm):
    i = pl.program_id(0)
    j = pl.program_id(1)

    @pl.when(j == 0)
    def _():
        g = g_ref[...]
        halo = _rms(hp_ref[...], g)
        a_sc[0:CONV_HALO, :] = jnp.where(i == 0, 0.0, halo).astype(BF16)
        a_sc[CONV_HALO:, :] = _rms(h_ref[...], g).astype(BF16)
        o_ref[...] = h_ref[...]

    a = a_sc[...]

    def conv(w_ref, cw_ref, cb_ref):
        u = _dot(a, w_ref[...])
        cw = cw_ref[...]
        c = cb_ref[...] + pltpu.roll(u, 2, 0)[CONV_HALO:] * cw[0:1]
        c = c + pltpu.roll(u, 1, 0)[CONV_HALO:] * cw[1:2]
        return c + u[CONV_HALO:] * cw[2:3]

    gate = conv(wg_ref, cwg_ref, cbg_ref)
    val = conv(wv_ref, cwv_ref, cbv_ref)
    act = (gate / (1.0 + jnp.exp(-gate))) * val
    o_ref[...] += _dot(act.astype(BF16), wo_ref[...])


def _ffn(h, gain, w_in, conv_w, conv_b, w_out, tm=512, tf=512):
    t, dm = h.shape
    ff = w_out.shape[0]
    tm = min(tm, t)
    nf = ff // tf
    halo_blocks = tm // CONV_HALO
    return pl.pallas_call(
        functools.partial(_ffn_kernel, tm=tm),
        out_shape=jax.ShapeDtypeStruct((t, dm), F32),
        grid=(t // tm, nf),
        in_specs=[
            pl.BlockSpec((tm, dm), lambda i, j: (i, 0)),
            pl.BlockSpec((CONV_HALO, dm), lambda i, j: (jnp.maximum(i * halo_blocks - 1, 0), 0)),
            pl.BlockSpec((1, dm), lambda i, j: (0, 0)),
            pl.BlockSpec((dm, tf), lambda i, j: (0, j)),
            pl.BlockSpec((dm, tf), lambda i, j: (0, j + nf)),
            pl.BlockSpec((CONV_WIDTH, tf), lambda i, j: (0, j)),
            pl.BlockSpec((CONV_WIDTH, tf), lambda i, j: (0, j + nf)),
            pl.BlockSpec((1, tf), lambda i, j: (0, j)),
            pl.BlockSpec((1, tf), lambda i, j: (0, j + nf)),
            pl.BlockSpec((tf, dm), lambda i, j: (j, 0)),
        ],
        out_specs=pl.BlockSpec((tm, dm), lambda i, j: (i, 0)),
        scratch_shapes=[pltpu.VMEM((tm + CONV_HALO, dm), BF16)],
        compiler_params=_cparams("arbitrary", "arbitrary"),
        name="conv_ffn",
    )(h, h, gain.reshape(1, dm), w_in, w_in, conv_w, conv_w, conv_b.reshape(1, -1), conv_b.reshape(1, -1), w_out)


def _norm_kernel(x_ref, g_ref, o_ref):
    o_ref[...] = _rms(x_ref[...], g_ref[...])


def _final_norm(h, gain, tm=512):
    t, dm = h.shape
    tm = min(tm, t)
    row = pl.BlockSpec((tm, dm), lambda i: (i, 0))
    return pl.pallas_call(
        _norm_kernel,
        out_shape=jax.ShapeDtypeStruct((t, dm), F32),
        grid=(t // tm,),
        in_specs=[row, pl.BlockSpec((1, dm), lambda i: (0, 0))],
        out_specs=row,
        compiler_params=_cparams("arbitrary"),
        name="final_norm",
    )(h, gain.reshape(1, dm))


def _mla(h, gain, positions, w_in, g_q, g_kv, w_qb, w_kvb, w_o, tm=512):
    t, dm = h.shape
    tm = min(tm, t)
    hh = MLA_HEADS
    qk = MLA_NOPE + MLA_ROPE
    lat_w = MLA_Q_RANK + MLA_KV_RANK + LANES
    cos_t, sin_t = _rope_tables(positions)
    w_in_p = jnp.pad(w_in, ((0, 0), (0, lat_w - w_in.shape[1]))).astype(BF16)
    w_qb_p = jnp.pad(w_qb.reshape(MLA_Q_RANK, hh, qk), ((0, 0), (0, 0), (0, 2 * LANES - qk)))
    w_qb_p = w_qb_p.reshape(MLA_Q_RANK, hh * 2 * LANES).astype(BF16)

    row128 = pl.BlockSpec((tm, LANES), lambda i, j: (i, 0))

    def lat_epilogue(y, extras, outs):
        gq_ref, gkv_ref, c_ref, s_ref = extras
        cq_ref, ckv_ref, kr_ref = outs
        cq_ref[...] = _rms(y[:, :MLA_Q_RANK], gq_ref[...]).astype(BF16)
        ckv_ref[...] = _rms(y[:, MLA_Q_RANK:MLA_Q_RANK + MLA_KV_RANK], gkv_ref[...]).astype(BF16)
        kr_ref[...] = _rope128(y[:, MLA_Q_RANK + MLA_KV_RANK:], c_ref[...], s_ref[...]).astype(BF16)

    rank_row = pl.BlockSpec((tm, MLA_Q_RANK), lambda i, j: (i, 0))
    c_q, c_kv, k_rope = _proj(
        "mla_latents", h, gain, w_in_p,
        extras=[(g_q.reshape(1, -1), pl.BlockSpec((1, MLA_Q_RANK), lambda i, j: (0, 0))),
                (g_kv.reshape(1, -1), pl.BlockSpec((1, MLA_KV_RANK), lambda i, j: (0, 0))),
                (cos_t, row128), (sin_t, row128)],
        outs=[((t, MLA_Q_RANK), BF16, rank_row), ((t, MLA_KV_RANK), BF16, rank_row), ((t, LANES), BF16, row128)],
        epilogue=lat_epilogue, tm=tm, tn=lat_w)

    heads_per_tile = 2
    tn = heads_per_tile * 2 * LANES
    scale = qk ** -0.5 * LOG2E

    def q_epilogue(y, extras, outs):
        c_ref, s_ref = extras
        (q_ref,) = outs
        y = y * scale
        for a in range(heads_per_tile):
            base = a * 2 * LANES
            q_ref[a, :, :LANES] = y[:, base:base + LANES].astype(BF16)
            q_ref[a, :, LANES:] = _rope128(y[:, base + LANES:base + 2 * LANES], c_ref[...], s_ref[...]).astype(BF16)

    head_blk = pl.BlockSpec((heads_per_tile, tm, 2 * LANES), lambda i, j: (j, i, 0))
    (q_cat,) = _proj(
        "mla_q", c_q, None, w_qb_p,
        extras=[(cos_t, row128), (sin_t, row128)],
        outs=[((hh, t, 2 * LANES), BF16, head_blk)],
        epilogue=q_epilogue, tm=tm, tn=tn)

    def kv_epilogue(y, extras, outs):
        (kr_ref,) = extras
        k_ref, v_ref = outs
        for a in range(heads_per_tile):
            base = a * 2 * LANES
            k_ref[a, :, :LANES] = y[:, base:base + LANES].astype(BF16)
            k_ref[a, :, LANES:] = kr_ref[...]
            v_ref[a, :, :LANES] = y[:, base + LANES:base + 2 * LANES].astype(BF16)
            v_ref[a, :, LANES:] = jnp.ones((tm, LANES), BF16)

    k_cat, v = _proj(
        "mla_kv", c_kv, None, w_kvb.astype(BF16),
        extras=[(k_rope, row128)],
        outs=[((hh, t, 2 * LANES), BF16, head_blk), ((hh, t, 2 * LANES), BF16, head_blk)],
        epilogue=kv_epilogue, tm=tm, tn=tn)

    o = _flash("mla_attn", q_cat, k_cat, v, 0, 0, 0, hh)
    return _out_proj("mla_out", o, w_o.astype(BF16), jnp.zeros((1, dm), F32), h)


def _cols_epilogue(y, extras, outs):
    b_ref, sc_ref = extras
    (o_ref,) = outs
    o_ref[...] = ((y + b_ref[...]) * sc_ref[...]).astype(BF16)


def _swa(h, gain, rel_bias, w_qkv, b_qkv, sinks, w_o, b_o, tm=512, tn=512):
    t, dm = h.shape
    tm = min(tm, t)
    n = w_qkv.shape[1]
    nq = SWA_Q_HEADS * SWA_HEAD_DIM
    grp = SWA_Q_HEADS // SWA_KV_HEADS
    colscale = jnp.concatenate([jnp.full((nq,), SWA_HEAD_DIM ** -0.5 * LOG2E, F32), jnp.ones((n - nq,), F32)])[None]
    vec = pl.BlockSpec((1, tn), lambda i, j: (0, j))
    (qkv,) = _proj(
        "swa_qkv", h, gain, w_qkv.astype(BF16),
        extras=[(b_qkv.reshape(1, n), vec), (colscale, vec)],
        outs=[((t, n), BF16, pl.BlockSpec((tm, tn), lambda i, j: (i, j)))],
        epilogue=_cols_epilogue, tm=tm, tn=tn)
    tab = _band_table(rel_bias, 1, SWA_WINDOW - 1)
    sink_rep = jnp.repeat(sinks.astype(F32) * LOG2E, SWA_HEAD_DIM)[None]
    k0 = nq // LANES
    v0 = k0 + SWA_KV_HEADS * SWA_HEAD_DIM // LANES
    (o,) = _band("swa_attn", qkv, lambda j: j, lambda j: k0 + j, lambda j: v0 + j, SWA_KV_HEADS // 2, 1, grp,
                 tab, lambda j: j, sink_rep=sink_rep, tile=256)
    return _out_proj("swa_out", o, w_o.astype(BF16), b_o.reshape(1, dm), h)


def _dilated(h, gain, rel_bias, w_qkv, w_o, tm=512, tn=512):
    t, dm = h.shape
    tm = min(tm, t)
    n = w_qkv.shape[1]
    nq = DIL_HEADS * DIL_HEAD_DIM
    colscale = jnp.concatenate([jnp.full((nq,), DIL_HEAD_DIM ** -0.5 * LOG2E, F32), jnp.ones((n - nq,), F32)])[None]
    vec = pl.BlockSpec((1, tn), lambda i, j: (0, j))
    (qkv,) = _proj(
        "dil_qkv", h, gain, w_qkv.astype(BF16),
        extras=[(jnp.zeros((1, n), F32), vec), (colscale, vec)],
        outs=[((t, n), BF16, pl.BlockSpec((tm, tn), lambda i, j: (i, j)))],
        epilogue=_cols_epilogue, tm=tm, tn=tn)
    npairs = 2
    groups = DIL_HEADS // (2 * npairs)
    cols = n // (LANES * npairs)
    os_, ms, ls = [], [], []
    for window, rate in DIL_PAIRS:
        n_keys = window // rate
        tab = _band_table(rel_bias, rate, n_keys)
        x = qkv.reshape(t // rate, rate * n)
        o, m, l = _band(
            f"dil_attn_r{rate}", x,
            lambda j: (j // groups) * cols + j % groups,
            lambda j: (j // groups) * cols + groups + j % groups,
            lambda j: (j // groups) * cols + 2 * groups + j % groups,
            rate * groups, npairs, 1, tab, lambda j: j % groups)
        os_.append(o.reshape(t, nq))
        ms.append(m[:, :rate * DIL_HEADS].reshape(t, DIL_HEADS))
        ls.append(l[:, :rate * DIL_HEADS].reshape(t, DIL_HEADS))
    pad = lambda a: jnp.pad(a, ((0, 0), (0, LANES - DIL_HEADS)))
    ms = [pad(m) for m in ms]
    ls = [pad(l) for l in ls]
    expand = jnp.repeat(jnp.eye(LANES, DIL_HEADS, dtype=BF16), DIL_HEAD_DIM, axis=1)
    return _dil_out(os_, ms, ls, expand, w_o.astype(BF16), h)


def _fox(h, gain, w_in, b_f, w_o, tm=512, tn=512):
    t, dm = h.shape
    tm = min(tm, t)
    hh = FOX_HEADS
    hd = hh * FOX_HEAD_DIM
    heads_per_tile = tn // FOX_HEAD_DIM
    colscale = jnp.concatenate([jnp.full((hd,), FOX_HEAD_DIM ** -0.5 * LOG2E, F32), jnp.ones((hd,), F32)])[None]

    def qk_epilogue(y, extras, outs):
        (sc_ref,) = extras
        (o_ref,) = outs
        y = y * sc_ref[...]
        for a in range(heads_per_tile):
            o_ref[a] = y[:, a * FOX_HEAD_DIM:(a + 1) * FOX_HEAD_DIM].astype(BF16)

    (qk,) = _proj(
        "fox_qk", h, gain, w_in[:, :2 * hd].astype(BF16),
        extras=[(colscale, pl.BlockSpec((1, tn), lambda i, j: (0, j)))],
        outs=[((2 * hh, t, FOX_HEAD_DIM), BF16,
               pl.BlockSpec((heads_per_tile, tm, FOX_HEAD_DIM), lambda i, j: (j, i, 0)))],
        epilogue=qk_epilogue, tm=tm, tn=tn)

    def v_epilogue(y, extras, outs):
        (o_ref,) = outs
        for a in range(heads_per_tile):
            o_ref[a, :, :LANES] = y[:, a * FOX_HEAD_DIM:(a + 1) * FOX_HEAD_DIM].astype(BF16)
            o_ref[a, :, LANES:] = jnp.ones((tm, LANES), BF16)

    (v_aug,) = _proj(
        "fox_v", h, gain, w_in[:, 2 * hd:3 * hd].astype(BF16),
        extras=[],
        outs=[((hh, t, 2 * LANES), BF16, pl.BlockSpec((heads_per_tile, tm, 2 * LANES), lambda i, j: (j, i, 0)))],
        epilogue=v_epilogue, tm=tm, tn=tn)

    def gate_epilogue(y, extras, outs):
        (b_ref,) = extras
        (o_ref,) = outs
        x = y + b_ref[...]
        o_ref[...] = jnp.minimum(x, 0.0) - jnp.log(1.0 + jnp.exp(-jnp.abs(x)))

    w_gate = jnp.pad(w_in[:, 3 * hd:], ((0, 0), (0, LANES - hh))).astype(BF16)
    b_gate = jnp.pad(b_f.astype(F32), (0, LANES - hh))[None]
    (log_f,) = _proj(
        "fox_gate", h, gain, w_gate,
        extras=[(b_gate, pl.BlockSpec((1, LANES), lambda i, j: (0, 0)))],
        outs=[((t, LANES), F32, pl.BlockSpec((tm, LANES), lambda i, j: (i, 0)))],
        epilogue=gate_epilogue, tm=tm, tn=LANES)
    cum, cum_t = _cumsum(log_f)
    o = _flash("fox_attn", qk, qk, v_aug, 0, hh, 0, hh, cum=cum, cumt=cum_t)
    return _out_proj("fox_out", o, w_o.astype(BF16), jnp.zeros((1, dm), F32), h)


def kernel(x, positions, rel_bias, norm_mix, norm_ffn, mla_w_in, mla_g_q, mla_g_kv, mla_w_qb, mla_w_kvb, mla_w_o, swa_w_qkv, swa_b_qkv, swa_sinks, swa_w_o, swa_b_o, dil_w_qkv, dil_w_o, fox_w_in, fox_b_f, fox_w_o, ffn_w_in, ffn_conv_w, ffn_conv_b, ffn_w_out, final_norm):
    bsz, t, dm = x.shape
    assert bsz == 1
    depth = norm_mix.shape[0]
    h = x.reshape(t, dm)
    pos = positions.reshape(t)
    for i in range(depth):
        kind = i % 4
        r = i // 4
        if kind == 0:
            h = _mla(h, norm_mix[i], pos, mla_w_in[r], mla_g_q[r], mla_g_kv[r], mla_w_qb[r], mla_w_kvb[r], mla_w_o[r])
        elif kind == 1:
            h = _swa(h, norm_mix[i], rel_bias, swa_w_qkv[r], swa_b_qkv[r], swa_sinks[r], swa_w_o[r], swa_b_o[r])
        elif kind == 2:
            h = _dilated(h, norm_mix[i], rel_bias, dil_w_qkv[r], dil_w_o[r])
        else:
            h = _fox(h, norm_mix[i], fox_w_in[r], fox_b_f[r], fox_w_o[r])
        h = _ffn(h, norm_ffn[i], ffn_w_in[i].astype(BF16), ffn_conv_w[i], ffn_conv_b[i], ffn_w_out[i].astype(BF16))
    return _final_norm(h, final_norm).reshape(bsz, t, dm)
```

```python
import functools
import math

import jax
import jax.numpy as jnp
from jax import lax
from jax.experimental import pallas as pl
from jax.experimental.pallas import tpu as pltpu

F32 = jnp.float32
BF16 = jnp.bfloat16
EPS = 1e-6
NEG = -1e30
LOG2E = math.log2(math.e)

LANES = 128
VMEM_LIMIT_BYTES = 56 * 1024 * 1024
BLK = 128

ROPE_THETA = 10000.0
REL_BUCKETS = 32
REL_MAX_DIST = 2048
MLA_HEADS, MLA_Q_RANK, MLA_KV_RANK, MLA_NOPE, MLA_ROPE, MLA_V = 16, 512, 512, 128, 64, 128
SWA_Q_HEADS, SWA_KV_HEADS, SWA_HEAD_DIM, SWA_WINDOW = 32, 4, 64, 128
DIL_HEADS, DIL_HEAD_DIM = 32, 64
DIL_PAIRS = ((128, 1), (512, 4), (2048, 16))
FOX_HEADS, FOX_HEAD_DIM = 16, 128
CONV_WIDTH = 3
CONV_HALO = 16
FFN_SLAB = 512


def _cparams(*sem):
    return pltpu.CompilerParams(dimension_semantics=sem, vmem_limit_bytes=VMEM_LIMIT_BYTES)


def _rms(x, g):
    return x * lax.rsqrt(jnp.mean(x * x, axis=-1, keepdims=True) + EPS) * g


def _dot(a, b):
    return jnp.dot(a, b, preferred_element_type=F32)


def _dot_nt(a, b):
    return lax.dot_general(a, b, (((1,), (1,)), ((), ())), preferred_element_type=F32)


def _rope_tab_kernel(pos_ref, f_ref, sgn_ref, c_ref, s_ref):
    ang = pos_ref[...].astype(F32) * f_ref[...]
    c_ref[...] = jnp.cos(ang)
    s_ref[...] = jnp.sin(ang) * sgn_ref[...]


def _rope_tables(positions):
    t = positions.shape[0]
    half = MLA_ROPE // 2
    inv_freq = ROPE_THETA ** (-jnp.arange(half, dtype=F32) / half)
    zeros = jnp.zeros((LANES - 2 * half,), F32)
    freq = jnp.concatenate([inv_freq, inv_freq, zeros])[None]
    sgn = jnp.concatenate([-jnp.ones((half,), F32), jnp.ones((half,), F32), zeros])[None]
    tm = min(t, 1024)
    row = pl.BlockSpec((tm, LANES), lambda i: (i, 0))
    vec = pl.BlockSpec((1, LANES), lambda i: (0, 0))
    return pl.pallas_call(
        _rope_tab_kernel,
        out_shape=(jax.ShapeDtypeStruct((t, LANES), F32),) * 2,
        grid=(t // tm,),
        in_specs=[pl.BlockSpec((tm, 1), lambda i: (i, 0)), vec, vec],
        out_specs=(row, row),
        compiler_params=_cparams("arbitrary"),
        name="rope_tables",
    )(positions.reshape(t, 1), freq, sgn)


def _rope128(z, c, s):
    half = MLA_ROPE // 2
    lane = lax.broadcasted_iota(jnp.int32, z.shape, 1)
    swapped = jnp.where(lane < half, pltpu.roll(z, LANES - half, 1), pltpu.roll(z, half, 1))
    return z * c + swapped * s


def _proj_kernel(*refs, norm, n_extra, epilogue):
    if norm:
        lhs_ref, g_ref, w_ref = refs[:3]
        rest = refs[3:]
        a_ref = rest[-1]
        rest = rest[:-1]

        @pl.when(pl.program_id(1) == 0)
        def _():
            a_ref[...] = _rms(lhs_ref[...], g_ref[...]).astype(BF16)

        a = a_ref[...]
    else:
        lhs_ref, w_ref = refs[:2]
        rest = refs[2:]
        a = lhs_ref[...]
    y = _dot(a, w_ref[...])
    epilogue(y, rest[:n_extra], rest[n_extra:])


def _proj(name, lhs, gain, w, extras, outs, epilogue, tm, tn):
    m, k = lhs.shape
    n = w.shape[1]
    norm = gain is not None
    in_specs = [pl.BlockSpec((tm, k), lambda i, j: (i, 0))]
    args = [lhs]
    if norm:
        in_specs.append(pl.BlockSpec((1, k), lambda i, j: (0, 0)))
        args.append(gain.reshape(1, k))
    in_specs.append(pl.BlockSpec((k, tn), lambda i, j: (0, j)))
    args.append(w)
    for arr, spec in extras:
        in_specs.append(spec)
        args.append(arr)
    res = pl.pallas_call(
        functools.partial(_proj_kernel, norm=norm, n_extra=len(extras), epilogue=epilogue),
        out_shape=tuple(jax.ShapeDtypeStruct(s, d) for s, d, _ in outs),
        grid=(m // tm, n // tn),
        in_specs=in_specs,
        out_specs=tuple(spec for _, _, spec in outs),
        scratch_shapes=[pltpu.VMEM((tm, k), BF16)] if norm else [],
        compiler_params=_cparams("arbitrary", "arbitrary"),
        name=name,
    )(*args)
    return res


def _out_proj_kernel(lhs_ref, w_ref, b_ref, res_ref, o_ref):
    o_ref[...] = res_ref[...] + (_dot(lhs_ref[...], w_ref[...]) + b_ref[...])


def _out_proj(name, lhs, w, bias, res, tm=512):
    m, k = lhs.shape
    n = w.shape[1]
    tm = min(tm, m)
    return pl.pallas_call(
        _out_proj_kernel,
        out_shape=jax.ShapeDtypeStruct((m, n), F32),
        grid=(m // tm,),
        in_specs=[
            pl.BlockSpec((tm, k), lambda i: (i, 0)),
            pl.BlockSpec((k, n), lambda i: (0, 0)),
            pl.BlockSpec((1, n), lambda i: (0, 0)),
            pl.BlockSpec((tm, n), lambda i: (i, 0)),
        ],
        out_specs=pl.BlockSpec((tm, n), lambda i: (i, 0)),
        compiler_params=_cparams("arbitrary"),
        name=name,
    )(lhs, w, bias, res)


def _flash_kernel(*refs, bq, fox):
    if fox:
        q_ref, k_ref, v_ref, cum_ref, cumt_ref, o_ref, s0_sc, s1_sc, m_sc, acc_sc = refs
    else:
        q_ref, k_ref, v_ref, o_ref, s0_sc, s1_sc, m_sc, acc_sc = refs
    h = pl.program_id(0)
    i = pl.program_id(1)
    reps = bq // LANES
    m_sc[...] = jnp.full(m_sc.shape, NEG, F32)
    acc_sc[...] = jnp.zeros(acc_sc.shape, F32)
    q = q_ref[0]
    if fox:
        lane = lax.broadcasted_iota(jnp.int32, (bq, LANES), 1)
        cq = jnp.sum(jnp.where(lane == h, cum_ref[...], 0.0), axis=1, keepdims=True)
        cq = jnp.broadcast_to(cq, (bq, LANES))

    def scores(c, dst):
        k0 = pl.multiple_of(c * bq, bq)
        dst[...] = _dot_nt(q, k_ref[0, pl.ds(k0, bq), :])

    def softmax_pv(c, src, masked):
        k0 = pl.multiple_of(c * bq, bq)
        t = src[...]
        if fox:
            t = t - cumt_ref[pl.ds(h, 1), pl.ds(k0, bq)]
        if masked:
            row = lax.broadcasted_iota(jnp.int32, (bq, bq), 0)
            col = lax.broadcasted_iota(jnp.int32, (bq, bq), 1)
            t = jnp.where(col <= row, t, NEG)
        r = jnp.max(t, axis=1, keepdims=True)
        m_prev = m_sc[...]
        m_new = jnp.maximum(m_prev, r + cq) if fox else jnp.maximum(m_prev, r)
        alpha = jnp.exp2(m_prev - m_new)
        shift = m_new - cq if fox else m_new
        p = jnp.exp2(t - jnp.tile(shift, (1, reps)))
        pv = _dot(p.astype(BF16), v_ref[0, pl.ds(k0, bq), :])
        acc_sc[...] = jnp.tile(alpha, (1, 2)) * acc_sc[...] + pv
        m_sc[...] = m_new

    scores(0, s0_sc)

    def body(u, carry):
        c = 2 * u
        scores(c + 1, s1_sc)
        softmax_pv(c, s0_sc, False)
        scores(c + 2, s0_sc)
        softmax_pv(c + 1, s1_sc, False)
        return carry

    lax.fori_loop(0, lax.shift_right_logical(i, 1), body, 0)

    @pl.when(lax.bitwise_and(i, 1) == 1)
    def _():
        scores(i, s1_sc)
        softmax_pv(i - 1, s0_sc, False)
        softmax_pv(i, s1_sc, True)

    @pl.when(lax.bitwise_and(i, 1) == 0)
    def _():
        softmax_pv(i, s0_sc, True)

    acc = acc_sc[...]
    o_ref[...] = (acc[:, :LANES] / acc[:, LANES:]).astype(BF16)


def _flash(name, q, k, v, q_off, k_off, v_off, heads, cum=None, cumt=None, bq=512):
    t = q.shape[1]
    dk = q.shape[2]
    dv = v.shape[2]
    assert dv == 2 * LANES
    bq = min(bq, t)
    fox = cum is not None
    in_specs = [
        pl.BlockSpec((1, bq, dk), lambda h, i: (q_off + h, i, 0)),
        pl.BlockSpec((1, t, dk), lambda h, i: (k_off + h, 0, 0)),
        pl.BlockSpec((1, t, dv), lambda h, i: (v_off + h, 0, 0)),
    ]
    args = [q, k, v]
    if fox:
        in_specs += [
            pl.BlockSpec((bq, LANES), lambda h, i: (i, 0)),
            pl.BlockSpec(cumt.shape, lambda h, i: (0, 0)),
        ]
        args += [cum, cumt]
    return pl.pallas_call(
        functools.partial(_flash_kernel, bq=bq, fox=fox),
        out_shape=jax.ShapeDtypeStruct((t, heads * LANES), BF16),
        grid=(heads, t // bq),
        in_specs=in_specs,
        out_specs=pl.BlockSpec((bq, LANES), lambda h, i: (i, h)),
        scratch_shapes=[pltpu.VMEM((bq, bq), F32), pltpu.VMEM((bq, bq), F32),
                        pltpu.VMEM((bq, LANES), F32), pltpu.VMEM((bq, dv), F32)],
        compiler_params=_cparams("arbitrary", "arbitrary"),
        name=name,
    )(*args)


def _band_kernel(*refs, nbk, npairs, grp, sinks):
    if sinks:
        q_ref, k_ref, v_ref, kp_ref, vp_ref, tab_ref, sink_ref, o_ref, kf_sc, vf_sc = refs
    else:
        q_ref, k_ref, v_ref, kp_ref, vp_ref, tab_ref, o_ref, m_ref, l_ref, kf_sc, vf_sc = refs
    i = pl.program_id(0)
    j = pl.program_id(1)
    kf_sc[0:BLK, :] = kp_ref[...]
    kf_sc[BLK:, :] = k_ref[...]
    vf_sc[0:BLK, :] = vp_ref[...]
    vf_sc[BLK:, :] = v_ref[...]
    d = LANES // 2
    n_kv = 2 * npairs
    col = lax.broadcasted_iota(jnp.int32, (1, 2 * BLK), 1)
    penrow = jnp.where(jnp.logical_and(col < BLK, i == 0), NEG, 0.0)
    if not sinks:
        stat_lane = lax.broadcasted_iota(jnp.int32, (BLK, LANES), 1)
        steps_per_stat = LANES // n_kv

        @pl.when(j % steps_per_stat == 0)
        def _():
            m_ref[...] = jnp.zeros(m_ref.shape, F32)
            l_ref[...] = jnp.zeros(l_ref.shape, F32)

    lo_k = lax.broadcasted_iota(jnp.int32, (2 * BLK, LANES), 1) < d
    lo_o = lax.broadcasted_iota(jnp.int32, (BLK, LANES), 1) < d
    swap = lambda a: jnp.concatenate([a[:, d:], a[:, :d]], axis=1)

    for b in range(nbk):
        r0 = b * BLK
        if not sinks:
            m_st = m_ref[r0:r0 + BLK, :]
            l_st = l_ref[r0:r0 + BLK, :]
        for kp in range(npairs):
            kk = kf_sc[r0:r0 + 2 * BLK, kp * LANES:(kp + 1) * LANES]
            vv = vf_sc[r0:r0 + 2 * BLK, kp * LANES:(kp + 1) * LANES]
            zero = jnp.zeros_like(kk)
            k_own = [jnp.where(lo_k, kk, zero), jnp.where(lo_k, zero, kk)]
            if grp > 1:
                k_by = [[k_own[0], swap(k_own[0])], [swap(k_own[1]), k_own[1]]]
                v_by = [[vv, swap(vv)], [swap(vv), vv]]
            else:
                k_by = [[k_own[0], None], [None, k_own[1]]]
                v_by = [[vv, None], [None, vv]]
            for c in range(grp):
                qcol = kp * grp + c
                qq = q_ref[r0:r0 + BLK, qcol * LANES:(qcol + 1) * LANES]
                halves = []
                for qh in range(2):
                    hq = 2 * qcol + qh
                    kh = hq // grp - 2 * kp
                    s = _dot_nt(qq, k_by[kh][qh]) + tab_ref[hq]
                    if b == 0:
                        s = s + penrow
                    m = jnp.max(s, axis=1, keepdims=True)
                    p = jnp.exp2(s - m)
                    l = jnp.sum(p, axis=1, keepdims=True)
                    acc = _dot(p.astype(BF16), v_by[kh][qh])
                    if sinks:
                        sink = sink_ref[:, qcol * LANES:(qcol + 1) * LANES]
                        m2 = jnp.maximum(m, sink)
                        a = jnp.exp2(m - m2)
                        den = l * a + jnp.exp2(sink - m2)
                        halves.append(acc * (a / den))
                    else:
                        halves.append(acc / l)
                        sel = stat_lane == (n_kv * j + hq) % LANES
                        m_st = jnp.where(sel, m, m_st)
                        l_st = jnp.where(sel, l, l_st)
                o_ref[r0:r0 + BLK, qcol * LANES:(qcol + 1) * LANES] = jnp.where(lo_o, halves[0], halves[1]).astype(BF16)
        if not sinks:
            m_ref[r0:r0 + BLK, :] = m_st
            l_ref[r0:r0 + BLK, :] = l_st


def _band(name, x, q_blk, k_blk, v_blk, n_steps, npairs, grp, tab, tab_blk, sink_rep=None, tile=512):
    length = x.shape[0]
    tb = min(tile, length)
    nbk = tb // BLK
    kw = LANES * npairs
    qw = kw * grp
    n_kv = 2 * npairs
    n_out = n_steps * qw
    sinks = sink_rep is not None
    prev = lambda i, j: jnp.maximum(i * nbk - 1, 0)
    in_specs = [
        pl.BlockSpec((tb, qw), lambda i, j: (i, q_blk(j))),
        pl.BlockSpec((tb, kw), lambda i, j: (i, k_blk(j))),
        pl.BlockSpec((tb, kw), lambda i, j: (i, v_blk(j))),
        pl.BlockSpec((BLK, kw), lambda i, j: (prev(i, j), k_blk(j))),
        pl.BlockSpec((BLK, kw), lambda i, j: (prev(i, j), v_blk(j))),
        pl.BlockSpec((n_kv * grp, BLK, 2 * BLK), lambda i, j: (tab_blk(j), 0, 0)),
    ]
    args = [x, x, x, x, x, tab]
    out_shape = [jax.ShapeDtypeStruct((length, n_out), BF16)]
    out_specs = [pl.BlockSpec((tb, qw), lambda i, j: (i, j))]
    if sinks:
        in_specs.append(pl.BlockSpec((1, qw), lambda i, j: (0, j)))
        args.append(sink_rep)
    else:
        n_stat = max(LANES, n_kv * n_steps)
        stat = pl.BlockSpec((tb, LANES), lambda i, j: (i, j // (LANES // n_kv)))
        out_shape += [jax.ShapeDtypeStruct((length, n_stat), F32)] * 2
        out_specs += [stat, stat]
    return pl.pallas_call(
        functools.partial(_band_kernel, nbk=nbk, npairs=npairs, grp=grp, sinks=sinks),
        out_shape=tuple(out_shape),
        grid=(length // tb, n_steps),
        in_specs=in_specs,
        out_specs=tuple(out_specs),
        scratch_shapes=[pltpu.VMEM((tb + BLK, kw), BF16), pltpu.VMEM((tb + BLK, kw), BF16)],
        compiler_params=_cparams("arbitrary", "arbitrary"),
        name=name,
    )(*args)


def _t5_bucket(n):
    exact = REL_BUCKETS // 2
    nf = jnp.maximum(n, 1).astype(F32)
    large = exact + (jnp.log(nf / exact) / math.log(REL_MAX_DIST / exact) * (REL_BUCKETS - exact)).astype(jnp.int32)
    return jnp.where(n < exact, n, jnp.minimum(large, REL_BUCKETS - 1))


def _band_table(rel_bias, rate, max_dist):
    nh = rel_bias.shape[1]
    w = 2 * BLK
    dist = jnp.arange(w)
    row = rel_bias[_t5_bucket(rate * jnp.minimum(dist, max_dist))].T * LOG2E
    f = jnp.where(dist[None] <= max_dist, row, NEG).astype(F32)
    r = jnp.roll(f[:, ::-1], BLK + 1, axis=1)
    rr = jnp.concatenate([r, r], axis=1)
    flat = jnp.broadcast_to(rr[:, None, :], (nh, BLK + 1, 2 * w)).reshape(nh, (BLK + 1) * 2 * w)
    return flat[:, w:w + BLK * (2 * w - 1)].reshape(nh, BLK, 2 * w - 1)[:, :, :w]


def _dil_out_kernel(o1, o2, o3, m1, m2, m3, l1, l2, l3, e_ref, w_ref, res_ref, out_ref):
    ms = [m1[...], m2[...], m3[...]]
    ls = [l1[...], l2[...], l3[...]]
    os_ = [o1, o2, o3]
    lane = lax.broadcasted_iota(jnp.int32, ms[0].shape, 1)
    mx = jnp.maximum(jnp.maximum(ms[0], ms[1]), ms[2])
    ws = [l * jnp.exp2(m - mx) for m, l in zip(ms, ls)]
    den = ws[0] + ws[1] + ws[2]
    e = e_ref[...]
    o = None
    for w, o_ref in zip(ws, os_):
        wn = jnp.where(lane < DIL_HEADS, w / den, 0.0)
        hi = wn.astype(BF16)
        lo = (wn - hi.astype(F32)).astype(BF16)
        term = o_ref[...].astype(F32) * (_dot(hi, e) + _dot(lo, e))
        o = term if o is None else o + term
    out_ref[...] = res_ref[...] + _dot(o.astype(BF16), w_ref[...])


def _dil_out(os_, ms, ls, expand, w, res, tm=512):
    t, n = res.shape
    tm = min(tm, t)
    row = pl.BlockSpec((tm, n), lambda i: (i, 0))
    stat = pl.BlockSpec((tm, LANES), lambda i: (i, 0))
    return pl.pallas_call(
        _dil_out_kernel,
        out_shape=jax.ShapeDtypeStruct((t, n), F32),
        grid=(t // tm,),
        in_specs=[row] * 3 + [stat] * 6 + [
            pl.BlockSpec((LANES, n), lambda i: (0, 0)),
            pl.BlockSpec((n, n), lambda i: (0, 0)),
            row,
        ],
        out_specs=row,
        compiler_params=_cparams("arbitrary"),
        name="dil_out",
    )(*os_, *ms, *ls, expand, w, res)


def _cumsum_kernel(x_ref, c_ref, ct_ref, carry_sc, *, tb):
    @pl.when(pl.program_id(0) == 0)
    def _():
        carry_sc[...] = jnp.zeros(carry_sc.shape, F32)

    x = x_ref[...]
    row = lax.broadcasted_iota(jnp.int32, (tb, tb), 0)
    colm = lax.broadcasted_iota(jnp.int32, (tb, tb), 1)
    tri = jnp.where(colm <= row, 1.0, 0.0).astype(BF16)
    hi = x.astype(BF16)
    r1 = x - hi.astype(F32)
    mid = r1.astype(BF16)
    lo = (r1 - mid.astype(F32)).astype(BF16)
    c = carry_sc[...] + (_dot(tri, hi) + _dot(tri, mid) + _dot(tri, lo))
    c2 = c * LOG2E
    c_ref[...] = c2
    ct_ref[...] = c2.T
    carry_sc[...] = c[tb - 1:tb, :]


def _cumsum(x, tb=256):
    t = x.shape[0]
    tb = min(tb, t)
    return pl.pallas_call(
        functools.partial(_cumsum_kernel, tb=tb),
        out_shape=(jax.ShapeDtypeStruct((t, LANES), F32), jax.ShapeDtypeStruct((LANES, t), F32)),
        grid=(t // tb,),
        in_specs=[pl.BlockSpec((tb, LANES), lambda i: (i, 0))],
        out_specs=(pl.BlockSpec((tb, LANES), lambda i: (i, 0)), pl.BlockSpec((LANES, tb), lambda i: (0, i))),
        scratch_shapes=[pltpu.VMEM((1, LANES), F32)],
        compiler_params=_cparams("arbitrary"),
        name="fox_cumsum",
    )(x)


def _ffn_kernel(h_ref, hp_ref, g_ref, wg_ref, wv_ref, cwg_ref, cwv_ref, cbg_ref, cbv_ref, wo_ref, o_ref, a_sc, *, tm):
    i = pl.program_id(0)
    j = pl.program_id(1)

    @pl.when(j == 0)
    def _():
        g = g_ref[...]
        halo = _rms(hp_ref[...], g)
        a_sc[0:CONV_HALO, :] = jnp.where(i == 0, 0.0, halo).astype(BF16)
        a_sc[CONV_HALO:, :] = _rms(h_ref[...], g).astype(BF16)
        o_ref[...] = h_ref[...]

    a = a_sc[...]

    def conv(w_ref, cw_ref, cb_ref, lo, hi):
        u = _dot(a, w_ref[:, lo:hi])
        cw = cw_ref[:, lo:hi]
        c = cb_ref[:, lo:hi] + pltpu.roll(u, 2, 0)[CONV_HALO:] * cw[0:1]
        c = c + pltpu.roll(u, 1, 0)[CONV_HALO:] * cw[1:2]
        return c + u[CONV_HALO:] * cw[2:3]

    tf = wo_ref.shape[0]
    contrib = None
    for lo in range(0, tf, FFN_SLAB):
        hi = lo + FFN_SLAB
        gate = conv(wg_ref, cwg_ref, cbg_ref, lo, hi)
        val = conv(wv_ref, cwv_ref, cbv_ref, lo, hi)
        act = (gate / (1.0 + jnp.exp(-gate))) * val
        part = _dot(act.astype(BF16), wo_ref[lo:hi, :])
        contrib = part if contrib is None else contrib + part
    o_ref[...] += contrib


def _ffn(h, gain, w_in, conv_w, conv_b, w_out, tm=1024, tf=512):
    t, dm = h.shape
    ff = w_out.shape[0]
    tm = min(tm, t)
    nf = ff // tf
    halo_blocks = tm // CONV_HALO
    once = pl.Buffered(1)
    return pl.pallas_call(
        functools.partial(_ffn_kernel, tm=tm),
        out_shape=jax.ShapeDtypeStruct((t, dm), F32),
        grid=(t // tm, nf),
        in_specs=[
            pl.BlockSpec((tm, dm), lambda i, j: (i, 0), pipeline_mode=once),
            pl.BlockSpec((CONV_HALO, dm), lambda i, j: (jnp.maximum(i * halo_blocks - 1, 0), 0)),
            pl.BlockSpec((1, dm), lambda i, j: (0, 0)),
            pl.BlockSpec((dm, tf), lambda i, j: (0, j)),
            pl.BlockSpec((dm, tf), lambda i, j: (0, j + nf)),
            pl.BlockSpec((CONV_WIDTH, tf), lambda i, j: (0, j)),
            pl.BlockSpec((CONV_WIDTH, tf), lambda i, j: (0, j + nf)),
            pl.BlockSpec((1, tf), lambda i, j: (0, j)),
            pl.BlockSpec((1, tf), lambda i, j: (0, j + nf)),
            pl.BlockSpec((tf, dm), lambda i, j: (j, 0)),
        ],
        out_specs=pl.BlockSpec((tm, dm), lambda i, j: (i, 0)),
        scratch_shapes=[pltpu.VMEM((tm + CONV_HALO, dm), BF16)],
        compiler_params=_cparams("arbitrary", "arbitrary"),
        name="conv_ffn",
    )(h, h, gain.reshape(1, dm), w_in, w_in, conv_w, conv_w, conv_b.reshape(1, -1), conv_b.reshape(1, -1), w_out)


def _norm_kernel(x_ref, g_ref, o_ref):
    o_ref[...] = _rms(x_ref[...], g_ref[...])


def _final_norm(h, gain, tm=512):
    t, dm = h.shape
    tm = min(tm, t)
    row = pl.BlockSpec((tm, dm), lambda i: (i, 0))
    return pl.pallas_call(
        _norm_kernel,
        out_shape=jax.ShapeDtypeStruct((t, dm), F32),
        grid=(t // tm,),
        in_specs=[row, pl.BlockSpec((1, dm), lambda i: (0, 0))],
        out_specs=row,
        compiler_params=_cparams("arbitrary"),
        name="final_norm",
    )(h, gain.reshape(1, dm))


def _mla(h, gain, positions, w_in, g_q, g_kv, w_qb, w_kvb, w_o, tm=512):
    t, dm = h.shape
    tm = min(tm, t)
    hh = MLA_HEADS
    qk = MLA_NOPE + MLA_ROPE
    lat_w = MLA_Q_RANK + MLA_KV_RANK + LANES
    cos_t, sin_t = _rope_tables(positions)
    w_in_p = jnp.pad(w_in, ((0, 0), (0, lat_w - w_in.shape[1]))).astype(BF16)
    w_qb_p = jnp.pad(w_qb.reshape(MLA_Q_RANK, hh, qk), ((0, 0), (0, 0), (0, 2 * LANES - qk)))
    w_qb_p = w_qb_p.reshape(MLA_Q_RANK, hh * 2 * LANES).astype(BF16)

    row128 = pl.BlockSpec((tm, LANES), lambda i, j: (i, 0))

    def lat_epilogue(y, extras, outs):
        gq_ref, gkv_ref, c_ref, s_ref = extras
        cq_ref, ckv_ref, kr_ref = outs
        cq_ref[...] = _rms(y[:, :MLA_Q_RANK], gq_ref[...]).astype(BF16)
        ckv_ref[...] = _rms(y[:, MLA_Q_RANK:MLA_Q_RANK + MLA_KV_RANK], gkv_ref[...]).astype(BF16)
        kr_ref[...] = _rope128(y[:, MLA_Q_RANK + MLA_KV_RANK:], c_ref[...], s_ref[...]).astype(BF16)

    rank_row = pl.BlockSpec((tm, MLA_Q_RANK), lambda i, j: (i, 0))
    c_q, c_kv, k_rope = _proj(
        "mla_latents", h, gain, w_in_p,
        extras=[(g_q.reshape(1, -1), pl.BlockSpec((1, MLA_Q_RANK), lambda i, j: (0, 0))),
                (g_kv.reshape(1, -1), pl.BlockSpec((1, MLA_KV_RANK), lambda i, j: (0, 0))),
                (cos_t, row128), (sin_t, row128)],
        outs=[((t, MLA_Q_RANK), BF16, rank_row), ((t, MLA_KV_RANK), BF16, rank_row), ((t, LANES), BF16, row128)],
        epilogue=lat_epilogue, tm=tm, tn=lat_w)

    heads_per_tile = 2
    tn = heads_per_tile * 2 * LANES
    scale = qk ** -0.5 * LOG2E

    def q_epilogue(y, extras, outs):
        c_ref, s_ref = extras
        (q_ref,) = outs
        y = y * scale
        for a in range(heads_per_tile):
            base = a * 2 * LANES
            q_ref[a, :, :LANES] = y[:, base:base + LANES].astype(BF16)
            q_ref[a, :, LANES:] = _rope128(y[:, base + LANES:base + 2 * LANES], c_ref[...], s_ref[...]).astype(BF16)

    head_blk = pl.BlockSpec((heads_per_tile, tm, 2 * LANES), lambda i, j: (j, i, 0))
    (q_cat,) = _proj(
        "mla_q", c_q, None, w_qb_p,
        extras=[(cos_t, row128), (sin_t, row128)],
        outs=[((hh, t, 2 * LANES), BF16, head_blk)],
        epilogue=q_epilogue, tm=tm, tn=tn)

    def kv_epilogue(y, extras, outs):
        (kr_ref,) = extras
        k_ref, v_ref = outs
        for a in range(heads_per_tile):
            base = a * 2 * LANES
            k_ref[a, :, :LANES] = y[:, base:base + LANES].astype(BF16)
            k_ref[a, :, LANES:] = kr_ref[...]
            v_ref[a, :, :LANES] = y[:, base + LANES:base + 2 * LANES].astype(BF16)
            v_ref[a, :, LANES:] = jnp.ones((tm, LANES), BF16)

    k_cat, v = _proj(
        "mla_kv", c_kv, None, w_kvb.astype(BF16),
        extras=[(k_rope, row128)],
        outs=[((hh, t, 2 * LANES), BF16, head_blk), ((hh, t, 2 * LANES), BF16, head_blk)],
        epilogue=kv_epilogue, tm=tm, tn=tn)

    o = _flash("mla_attn", q_cat, k_cat, v, 0, 0, 0, hh)
    return _out_proj("mla_out", o, w_o.astype(BF16), jnp.zeros((1, dm), F32), h)


def _cols_epilogue(y, extras, outs):
    b_ref, sc_ref = extras
    (o_ref,) = outs
    o_ref[...] = ((y + b_ref[...]) * sc_ref[...]).astype(BF16)


def _swa(h, gain, rel_bias, w_qkv, b_qkv, sinks, w_o, b_o, tm=512, tn=512):
    t, dm = h.shape
    tm = min(tm, t)
    n = w_qkv.shape[1]
    nq = SWA_Q_HEADS * SWA_HEAD_DIM
    grp = SWA_Q_HEADS // SWA_KV_HEADS
    colscale = jnp.concatenate([jnp.full((nq,), SWA_HEAD_DIM ** -0.5 * LOG2E, F32), jnp.ones((n - nq,), F32)])[None]
    vec = pl.BlockSpec((1, tn), lambda i, j: (0, j))
    (qkv,) = _proj(
        "swa_qkv", h, gain, w_qkv.astype(BF16),
        extras=[(b_qkv.reshape(1, n), vec), (colscale, vec)],
        outs=[((t, n), BF16, pl.BlockSpec((tm, tn), lambda i, j: (i, j)))],
        epilogue=_cols_epilogue, tm=tm, tn=tn)
    tab = _band_table(rel_bias, 1, SWA_WINDOW - 1)
    sink_rep = jnp.repeat(sinks.astype(F32) * LOG2E, SWA_HEAD_DIM)[None]
    k0 = nq // LANES
    v0 = k0 + SWA_KV_HEADS * SWA_HEAD_DIM // LANES
    (o,) = _band("swa_attn", qkv, lambda j: j, lambda j: k0 + j, lambda j: v0 + j, SWA_KV_HEADS // 2, 1, grp,
                 tab, lambda j: j, sink_rep=sink_rep, tile=BLK)
    return _out_proj("swa_out", o, w_o.astype(BF16), b_o.reshape(1, dm), h)


def _dilated(h, gain, rel_bias, w_qkv, w_o, tm=512, tn=512):
    t, dm = h.shape
    tm = min(tm, t)
    n = w_qkv.shape[1]
    nq = DIL_HEADS * DIL_HEAD_DIM
    colscale = jnp.concatenate([jnp.full((nq,), DIL_HEAD_DIM ** -0.5 * LOG2E, F32), jnp.ones((n - nq,), F32)])[None]
    vec = pl.BlockSpec((1, tn), lambda i, j: (0, j))
    (qkv,) = _proj(
        "dil_qkv", h, gain, w_qkv.astype(BF16),
        extras=[(jnp.zeros((1, n), F32), vec), (colscale, vec)],
        outs=[((t, n), BF16, pl.BlockSpec((tm, tn), lambda i, j: (i, j)))],
        epilogue=_cols_epilogue, tm=tm, tn=tn)
    npairs = 2
    groups = DIL_HEADS // (2 * npairs)
    cols = n // (LANES * npairs)
    os_, ms, ls = [], [], []
    for window, rate in DIL_PAIRS:
        n_keys = window // rate
        tab = _band_table(rel_bias, rate, n_keys)
        x = qkv.reshape(t // rate, rate * n)
        o, m, l = _band(
            f"dil_attn_r{rate}", x,
            lambda j: (j // groups) * cols + j % groups,
            lambda j: (j // groups) * cols + groups + j % groups,
            lambda j: (j // groups) * cols + 2 * groups + j % groups,
            rate * groups, npairs, 1, tab, lambda j: j % groups)
        os_.append(o.reshape(t, nq))
        ms.append(m[:, :rate * DIL_HEADS].reshape(t, DIL_HEADS))
        ls.append(l[:, :rate * DIL_HEADS].reshape(t, DIL_HEADS))
    pad = lambda a: jnp.pad(a, ((0, 0), (0, LANES - DIL_HEADS)))
    ms = [pad(m) for m in ms]
    ls = [pad(l) for l in ls]
    expand = jnp.repeat(jnp.eye(LANES, DIL_HEADS, dtype=BF16), DIL_HEAD_DIM, axis=1)
    return _dil_out(os_, ms, ls, expand, w_o.astype(BF16), h)


def _fox(h, gain, w_in, b_f, w_o, tm=512, tn=512):
    t, dm = h.shape
    tm = min(tm, t)
    hh = FOX_HEADS
    hd = hh * FOX_HEAD_DIM
    heads_per_tile = tn // FOX_HEAD_DIM
    colscale = jnp.concatenate([jnp.full((hd,), FOX_HEAD_DIM ** -0.5 * LOG2E, F32), jnp.ones((hd,), F32)])[None]

    def qk_epilogue(y, extras, outs):
        (sc_ref,) = extras
        (o_ref,) = outs
        y = y * sc_ref[...]
        for a in range(heads_per_tile):
            o_ref[a] = y[:, a * FOX_HEAD_DIM:(a + 1) * FOX_HEAD_DIM].astype(BF16)

    (qk,) = _proj(
        "fox_qk", h, gain, w_in[:, :2 * hd].astype(BF16),
        extras=[(colscale, pl.BlockSpec((1, tn), lambda i, j: (0, j)))],
        outs=[((2 * hh, t, FOX_HEAD_DIM), BF16,
               pl.BlockSpec((heads_per_tile, tm, FOX_HEAD_DIM), lambda i, j: (j, i, 0)))],
        epilogue=qk_epilogue, tm=tm, tn=tn)

    def v_epilogue(y, extras, outs):
        (o_ref,) = outs
        for a in range(heads_per_tile):
            o_ref[a, :, :LANES] = y[:, a * FOX_HEAD_DIM:(a + 1) * FOX_HEAD_DIM].astype(BF16)
            o_ref[a, :, LANES:] = jnp.ones((tm, LANES), BF16)

    (v_aug,) = _proj(
        "fox_v", h, gain, w_in[:, 2 * hd:3 * hd].astype(BF16),
        extras=[],
        outs=[((hh, t, 2 * LANES), BF16, pl.BlockSpec((heads_per_tile, tm, 2 * LANES), lambda i, j: (j, i, 0)))],
        epilogue=v_epilogue, tm=tm, tn=tn)

    def gate_epilogue(y, extras, outs):
        (b_ref,) = extras
        (o_ref,) = outs
        x = y + b_ref[...]
        o_ref[...] = jnp.minimum(x, 0.0) - jnp.log(1.0 + jnp.exp(-jnp.abs(x)))

    w_gate = jnp.pad(w_in[:, 3 * hd:], ((0, 0), (0, LANES - hh))).astype(BF16)
    b_gate = jnp.pad(b_f.astype(F32), (0, LANES - hh))[None]
    (log_f,) = _proj(
        "fox_gate", h, gain, w_gate,
        extras=[(b_gate, pl.BlockSpec((1, LANES), lambda i, j: (0, 0)))],
        outs=[((t, LANES), F32, pl.BlockSpec((tm, LANES), lambda i, j: (i, 0)))],
        epilogue=gate_epilogue, tm=tm, tn=LANES)
    cum, cum_t = _cumsum(log_f)
    o = _flash("fox_attn", qk, qk, v_aug, 0, hh, 0, hh, cum=cum, cumt=cum_t)
    return _out_proj("fox_out", o, w_o.astype(BF16), jnp.zeros((1, dm), F32), h)


def kernel(x, positions, rel_bias, norm_mix, norm_ffn, mla_w_in, mla_g_q, mla_g_kv, mla_w_qb, mla_w_kvb, mla_w_o, swa_w_qkv, swa_b_qkv, swa_sinks, swa_w_o, swa_b_o, dil_w_qkv, dil_w_o, fox_w_in, fox_b_f, fox_w_o, ffn_w_in, ffn_conv_w, ffn_conv_b, ffn_w_out, final_norm):
    bsz, t, dm = x.shape
    assert bsz == 1
    depth = norm_mix.shape[0]
    h = x.reshape(t, dm)
    pos = positions.reshape(t)
    for i in range(depth):
        kind = i % 4
        r = i // 4
        if kind == 0:
            h = _mla(h, norm_mix[i], pos, mla_w_in[r], mla_g_q[r], mla_g_kv[r], mla_w_qb[r], mla_w_kvb[r], mla_w_o[r])
        elif kind == 1:
            h = _swa(h, norm_mix[i], rel_bias, swa_w_qkv[r], swa_b_qkv[r], swa_sinks[r], swa_w_o[r], swa_b_o[r])
        elif kind == 2:
            h = _dilated(h, norm_mix[i], rel_bias, dil_w_qkv[r], dil_w_o[r])
        else:
            h = _fox(h, norm_mix[i], fox_w_in[r], fox_b_f[r], fox_w_o[r])
        h = _ffn(h, norm_ffn[i], ffn_w_in[i].astype(BF16), ffn_conv_w[i], ffn_conv_b[i], ffn_w_out[i].astype(BF16))
    return _final_norm(h, final_norm).reshape(bsz, t, dm)
```

```python
import functools
import math

import jax
import jax.numpy as jnp
from jax import lax
from jax.experimental import pallas as pl
from jax.experimental.pallas import tpu as pltpu

F32 = jnp.float32
BF16 = jnp.bfloat16
EPS = 1e-6
NEG = -1e30
LOG2E = math.log2(math.e)

LANES = 128
VMEM_LIMIT_BYTES = 56 * 1024 * 1024
BLK = 128

ROPE_THETA = 10000.0
REL_BUCKETS = 32
REL_MAX_DIST = 2048
MLA_HEADS, MLA_Q_RANK, MLA_KV_RANK, MLA_NOPE, MLA_ROPE, MLA_V = 16, 512, 512, 128, 64, 128
SWA_Q_HEADS, SWA_KV_HEADS, SWA_HEAD_DIM, SWA_WINDOW = 32, 4, 64, 128
DIL_HEADS, DIL_HEAD_DIM = 32, 64
DIL_PAIRS = ((128, 1), (512, 4), (2048, 16))
FOX_HEADS, FOX_HEAD_DIM = 16, 128
CONV_WIDTH = 3
CONV_HALO = 16
FFN_SLAB = 512


def _cparams(*sem):
    return pltpu.CompilerParams(dimension_semantics=sem, vmem_limit_bytes=VMEM_LIMIT_BYTES)


def _rms(x, g):
    return x * lax.rsqrt(jnp.mean(x * x, axis=-1, keepdims=True) + EPS) * g


def _dot(a, b):
    return jnp.dot(a, b, preferred_element_type=F32)


def _dot_nt(a, b):
    return lax.dot_general(a, b, (((1,), (1,)), ((), ())), preferred_element_type=F32)


def _rope_tab_kernel(pos_ref, f_ref, sgn_ref, c_ref, s_ref):
    ang = pos_ref[...].astype(F32) * f_ref[...]
    c_ref[...] = jnp.cos(ang)
    s_ref[...] = jnp.sin(ang) * sgn_ref[...]


def _rope_tables(positions):
    t = positions.shape[0]
    half = MLA_ROPE // 2
    inv_freq = ROPE_THETA ** (-jnp.arange(half, dtype=F32) / half)
    zeros = jnp.zeros((LANES - 2 * half,), F32)
    freq = jnp.concatenate([inv_freq, inv_freq, zeros])[None]
    sgn = jnp.concatenate([-jnp.ones((half,), F32), jnp.ones((half,), F32), zeros])[None]
    tm = min(t, 1024)
    row = pl.BlockSpec((tm, LANES), lambda i: (i, 0))
    vec = pl.BlockSpec((1, LANES), lambda i: (0, 0))
    return pl.pallas_call(
        _rope_tab_kernel,
        out_shape=(jax.ShapeDtypeStruct((t, LANES), F32),) * 2,
        grid=(t // tm,),
        in_specs=[pl.BlockSpec((tm, 1), lambda i: (i, 0)), vec, vec],
        out_specs=(row, row),
        compiler_params=_cparams("arbitrary"),
        name="rope_tables",
    )(positions.reshape(t, 1), freq, sgn)


def _rope128(z, c, s):
    half = MLA_ROPE // 2
    lane = lax.broadcasted_iota(jnp.int32, z.shape, 1)
    swapped = jnp.where(lane < half, pltpu.roll(z, LANES - half, 1), pltpu.roll(z, half, 1))
    return z * c + swapped * s


def _proj_kernel(*refs, norm, n_extra, epilogue):
    if norm:
        lhs_ref, g_ref, w_ref = refs[:3]
        rest = refs[3:]
        a_ref = rest[-1]
        rest = rest[:-1]

        @pl.when(pl.program_id(1) == 0)
        def _():
            a_ref[...] = _rms(lhs_ref[...], g_ref[...]).astype(BF16)

        a = a_ref[...]
    else:
        lhs_ref, w_ref = refs[:2]
        rest = refs[2:]
        a = lhs_ref[...]
    y = _dot(a, w_ref[...])
    epilogue(y, rest[:n_extra], rest[n_extra:])


def _proj(name, lhs, gain, w, extras, outs, epilogue, tm, tn, col0=0, n=None):
    m, k = lhs.shape
    n = w.shape[1] if n is None else n
    woff = col0 // tn
    norm = gain is not None
    in_specs = [pl.BlockSpec((tm, k), lambda i, j: (i, 0))]
    args = [lhs]
    if norm:
        in_specs.append(pl.BlockSpec((1, k), lambda i, j: (0, 0)))
        args.append(gain.reshape(1, k))
    in_specs.append(pl.BlockSpec((k, tn), lambda i, j: (0, woff + j)))
    args.append(w)
    for arr, spec in extras:
        in_specs.append(spec)
        args.append(arr)
    res = pl.pallas_call(
        functools.partial(_proj_kernel, norm=norm, n_extra=len(extras), epilogue=epilogue),
        out_shape=tuple(jax.ShapeDtypeStruct(s, d) for s, d, _ in outs),
        grid=(m // tm, n // tn),
        in_specs=in_specs,
        out_specs=tuple(spec for _, _, spec in outs),
        scratch_shapes=[pltpu.VMEM((tm, k), BF16)] if norm else [],
        compiler_params=_cparams("arbitrary", "arbitrary"),
        name=name,
    )(*args)
    return res


def _out_proj_kernel(lhs_ref, w_ref, b_ref, res_ref, o_ref):
    o_ref[...] = res_ref[...] + (_dot(lhs_ref[...], w_ref[...]) + b_ref[...])


def _out_proj(name, lhs, w, bias, res, tm=512):
    m, k = lhs.shape
    n = w.shape[1]
    tm = min(tm, m)
    return pl.pallas_call(
        _out_proj_kernel,
        out_shape=jax.ShapeDtypeStruct((m, n), F32),
        grid=(m // tm,),
        in_specs=[
            pl.BlockSpec((tm, k), lambda i: (i, 0)),
            pl.BlockSpec((k, n), lambda i: (0, 0)),
            pl.BlockSpec((1, n), lambda i: (0, 0)),
            pl.BlockSpec((tm, n), lambda i: (i, 0)),
        ],
        out_specs=pl.BlockSpec((tm, n), lambda i: (i, 0)),
        compiler_params=_cparams("arbitrary"),
        name=name,
    )(lhs, w, bias, res)


def _flash_kernel(*refs, bq, fox):
    if fox:
        q_ref, k_ref, v_ref, cum_ref, cumt_ref, o_ref, s0_sc, s1_sc, m_sc, acc_sc = refs
    else:
        q_ref, k_ref, v_ref, o_ref, s0_sc, s1_sc, m_sc, acc_sc = refs
    h = pl.program_id(0)
    i = pl.program_id(1)
    reps = bq // LANES
    m_sc[...] = jnp.full(m_sc.shape, NEG, F32)
    acc_sc[...] = jnp.zeros(acc_sc.shape, F32)
    q = q_ref[0]
    if fox:
        lane = lax.broadcasted_iota(jnp.int32, (bq, LANES), 1)
        cq = jnp.sum(jnp.where(lane == h, cum_ref[...], 0.0), axis=1, keepdims=True)
        cq = jnp.broadcast_to(cq, (bq, LANES))

    def scores(c, dst):
        k0 = pl.multiple_of(c * bq, bq)
        dst[...] = _dot_nt(q, k_ref[0, pl.ds(k0, bq), :])

    def softmax_pv(c, src, masked):
        k0 = pl.multiple_of(c * bq, bq)
        t = src[...]
        if fox:
            t = t - cumt_ref[pl.ds(h, 1), pl.ds(k0, bq)]
        if masked:
            row = lax.broadcasted_iota(jnp.int32, (bq, bq), 0)
            col = lax.broadcasted_iota(jnp.int32, (bq, bq), 1)
            t = jnp.where(col <= row, t, NEG)
        r = jnp.max(t, axis=1, keepdims=True)
        m_prev = m_sc[...]
        m_new = jnp.maximum(m_prev, r + cq) if fox else jnp.maximum(m_prev, r)
        alpha = jnp.exp2(m_prev - m_new)
        shift = m_new - cq if fox else m_new
        p = jnp.exp2(t - jnp.tile(shift, (1, reps)))
        pv = _dot(p.astype(BF16), v_ref[0, pl.ds(k0, bq), :])
        acc_sc[...] = jnp.tile(alpha, (1, 2)) * acc_sc[...] + pv
        m_sc[...] = m_new

    scores(0, s0_sc)

    def body(u, carry):
        c = 2 * u
        scores(c + 1, s1_sc)
        softmax_pv(c, s0_sc, False)
        scores(c + 2, s0_sc)
        softmax_pv(c + 1, s1_sc, False)
        return carry

    lax.fori_loop(0, lax.shift_right_logical(i, 1), body, 0)

    @pl.when(lax.bitwise_and(i, 1) == 1)
    def _():
        scores(i, s1_sc)
        softmax_pv(i - 1, s0_sc, False)
        softmax_pv(i, s1_sc, True)

    @pl.when(lax.bitwise_and(i, 1) == 0)
    def _():
        softmax_pv(i, s0_sc, True)

    acc = acc_sc[...]
    o_ref[...] = (acc[:, :LANES] / acc[:, LANES:]).astype(BF16)


def _flash(name, q, k, v, q_off, k_off, v_off, heads, cum=None, cumt=None, bq=512):
    t = q.shape[1]
    dk = q.shape[2]
    dv = v.shape[2]
    assert dv == 2 * LANES
    bq = min(bq, t)
    fox = cum is not None
    in_specs = [
        pl.BlockSpec((1, bq, dk), lambda h, i: (q_off + h, i, 0)),
        pl.BlockSpec((1, t, dk), lambda h, i: (k_off + h, 0, 0)),
        pl.BlockSpec((1, t, dv), lambda h, i: (v_off + h, 0, 0)),
    ]
    args = [q, k, v]
    if fox:
        in_specs += [
            pl.BlockSpec((bq, LANES), lambda h, i: (i, 0)),
            pl.BlockSpec(cumt.shape, lambda h, i: (0, 0)),
        ]
        args += [cum, cumt]
    return pl.pallas_call(
        functools.partial(_flash_kernel, bq=bq, fox=fox),
        out_shape=jax.ShapeDtypeStruct((t, heads * LANES), BF16),
        grid=(heads, t // bq),
        in_specs=in_specs,
        out_specs=pl.BlockSpec((bq, LANES), lambda h, i: (i, h)),
        scratch_shapes=[pltpu.VMEM((bq, bq), F32), pltpu.VMEM((bq, bq), F32),
                        pltpu.VMEM((bq, LANES), F32), pltpu.VMEM((bq, dv), F32)],
        compiler_params=_cparams("arbitrary", "arbitrary"),
        name=name,
    )(*args)


def _band_kernel(*refs, nbk, npairs, grp, sinks):
    if sinks:
        q_ref, k_ref, v_ref, kp_ref, vp_ref, tab_ref, sink_ref, o_ref, kf_sc, vf_sc = refs
    else:
        q_ref, k_ref, v_ref, kp_ref, vp_ref, tab_ref, o_ref, m_ref, l_ref, kf_sc, vf_sc = refs
    i = pl.program_id(0)
    j = pl.program_id(1)
    kf_sc[0:BLK, :] = kp_ref[...]
    kf_sc[BLK:, :] = k_ref[...]
    vf_sc[0:BLK, :] = vp_ref[...]
    vf_sc[BLK:, :] = v_ref[...]
    d = LANES // 2
    n_kv = 2 * npairs
    col = lax.broadcasted_iota(jnp.int32, (1, 2 * BLK), 1)
    penrow = jnp.where(jnp.logical_and(col < BLK, i == 0), NEG, 0.0)
    if not sinks:
        stat_lane = lax.broadcasted_iota(jnp.int32, (BLK, LANES), 1)
        steps_per_stat = LANES // n_kv

        @pl.when(j % steps_per_stat == 0)
        def _():
            m_ref[...] = jnp.zeros(m_ref.shape, F32)
            l_ref[...] = jnp.zeros(l_ref.shape, F32)

    lo_k = lax.broadcasted_iota(jnp.int32, (2 * BLK, LANES), 1) < d
    lo_o = lax.broadcasted_iota(jnp.int32, (BLK, LANES), 1) < d
    swap = lambda a: jnp.concatenate([a[:, d:], a[:, :d]], axis=1)

    for b in range(nbk):
        r0 = b * BLK
        if not sinks:
            m_st = m_ref[r0:r0 + BLK, :]
            l_st = l_ref[r0:r0 + BLK, :]
        for kp in range(npairs):
            kk = kf_sc[r0:r0 + 2 * BLK, kp * LANES:(kp + 1) * LANES]
            vv = vf_sc[r0:r0 + 2 * BLK, kp * LANES:(kp + 1) * LANES]
            zero = jnp.zeros_like(kk)
            k_own = [jnp.where(lo_k, kk, zero), jnp.where(lo_k, zero, kk)]
            if grp > 1:
                k_by = [[k_own[0], swap(k_own[0])], [swap(k_own[1]), k_own[1]]]
                v_by = [[vv, swap(vv)], [swap(vv), vv]]
            else:
                k_by = [[k_own[0], None], [None, k_own[1]]]
                v_by = [[vv, None], [None, vv]]
            for c in range(grp):
                qcol = kp * grp + c
                qq = q_ref[r0:r0 + BLK, qcol * LANES:(qcol + 1) * LANES]
                halves = []
                for qh in range(2):
                    hq = 2 * qcol + qh
                    kh = hq // grp - 2 * kp
                    s = _dot_nt(qq, k_by[kh][qh]) + tab_ref[hq]
                    if b == 0:
                        s = s + penrow
                    m = jnp.max(s, axis=1, keepdims=True)
                    p = jnp.exp2(s - m)
                    l = jnp.sum(p, axis=1, keepdims=True)
                    acc = _dot(p.astype(BF16), v_by[kh][qh])
                    if sinks:
                        sink = sink_ref[:, qcol * LANES:(qcol + 1) * LANES]
                        m2 = jnp.maximum(m, sink)
                        a = jnp.exp2(m - m2)
                        den = l * a + jnp.exp2(sink - m2)
                        halves.append(acc * (a / den))
                    else:
                        halves.append(acc / l)
                        sel = stat_lane == (n_kv * j + hq) % LANES
                        m_st = jnp.where(sel, m, m_st)
                        l_st = jnp.where(sel, l, l_st)
                o_ref[r0:r0 + BLK, qcol * LANES:(qcol + 1) * LANES] = jnp.where(lo_o, halves[0], halves[1]).astype(BF16)
        if not sinks:
            m_ref[r0:r0 + BLK, :] = m_st
            l_ref[r0:r0 + BLK, :] = l_st


def _band(name, x, q_blk, k_blk, v_blk, n_steps, npairs, grp, tab, tab_blk, sink_rep=None, tile=512):
    length = x.shape[0]
    tb = min(tile, length)
    nbk = tb // BLK
    kw = LANES * npairs
    qw = kw * grp
    n_kv = 2 * npairs
    n_out = n_steps * qw
    sinks = sink_rep is not None
    prev = lambda i, j: jnp.maximum(i * nbk - 1, 0)
    in_specs = [
        pl.BlockSpec((tb, qw), lambda i, j: (i, q_blk(j))),
        pl.BlockSpec((tb, kw), lambda i, j: (i, k_blk(j))),
        pl.BlockSpec((tb, kw), lambda i, j: (i, v_blk(j))),
        pl.BlockSpec((BLK, kw), lambda i, j: (prev(i, j), k_blk(j))),
        pl.BlockSpec((BLK, kw), lambda i, j: (prev(i, j), v_blk(j))),
        pl.BlockSpec((n_kv * grp, BLK, 2 * BLK), lambda i, j: (tab_blk(j), 0, 0)),
    ]
    args = [x, x, x, x, x, tab]
    out_shape = [jax.ShapeDtypeStruct((length, n_out), BF16)]
    out_specs = [pl.BlockSpec((tb, qw), lambda i, j: (i, j))]
    if sinks:
        in_specs.append(pl.BlockSpec((1, qw), lambda i, j: (0, j)))
        args.append(sink_rep)
    else:
        n_stat = max(LANES, n_kv * n_steps)
        stat = pl.BlockSpec((tb, LANES), lambda i, j: (i, j // (LANES // n_kv)))
        out_shape += [jax.ShapeDtypeStruct((length, n_stat), F32)] * 2
        out_specs += [stat, stat]
    return pl.pallas_call(
        functools.partial(_band_kernel, nbk=nbk, npairs=npairs, grp=grp, sinks=sinks),
        out_shape=tuple(out_shape),
        grid=(length // tb, n_steps),
        in_specs=in_specs,
        out_specs=tuple(out_specs),
        scratch_shapes=[pltpu.VMEM((tb + BLK, kw), BF16), pltpu.VMEM((tb + BLK, kw), BF16)],
        compiler_params=_cparams("arbitrary", "arbitrary"),
        name=name,
    )(*args)


def _t5_bucket(n):
    exact = REL_BUCKETS // 2
    nf = jnp.maximum(n, 1).astype(F32)
    large = exact + (jnp.log(nf / exact) / math.log(REL_MAX_DIST / exact) * (REL_BUCKETS - exact)).astype(jnp.int32)
    return jnp.where(n < exact, n, jnp.minimum(large, REL_BUCKETS - 1))


def _band_table(rel_bias, rate, max_dist):
    nh = rel_bias.shape[1]
    w = 2 * BLK
    dist = jnp.arange(w)
    row = rel_bias[_t5_bucket(rate * jnp.minimum(dist, max_dist))].T * LOG2E
    f = jnp.where(dist[None] <= max_dist, row, NEG).astype(F32)
    r = jnp.roll(f[:, ::-1], BLK + 1, axis=1)
    rr = jnp.concatenate([r, r], axis=1)
    flat = jnp.broadcast_to(rr[:, None, :], (nh, BLK + 1, 2 * w)).reshape(nh, (BLK + 1) * 2 * w)
    return flat[:, w:w + BLK * (2 * w - 1)].reshape(nh, BLK, 2 * w - 1)[:, :, :w]


def _dil_out_kernel(o1, o2, o3, m1, m2, m3, l1, l2, l3, e_ref, w_ref, res_ref, out_ref):
    ms = [m1[...], m2[...], m3[...]]
    ls = [l1[...], l2[...], l3[...]]
    os_ = [o1, o2, o3]
    lane = lax.broadcasted_iota(jnp.int32, ms[0].shape, 1)
    mx = jnp.maximum(jnp.maximum(ms[0], ms[1]), ms[2])
    ws = [l * jnp.exp2(m - mx) for m, l in zip(ms, ls)]
    den = ws[0] + ws[1] + ws[2]
    e = e_ref[...]
    o = None
    for w, o_ref in zip(ws, os_):
        wn = jnp.where(lane < DIL_HEADS, w / den, 0.0)
        hi = wn.astype(BF16)
        lo = (wn - hi.astype(F32)).astype(BF16)
        term = o_ref[...].astype(F32) * (_dot(hi, e) + _dot(lo, e))
        o = term if o is None else o + term
    out_ref[...] = res_ref[...] + _dot(o.astype(BF16), w_ref[...])


def _dil_out(os_, ms, ls, expand, w, res, tm=512):
    t, n = res.shape
    tm = min(tm, t)
    row = pl.BlockSpec((tm, n), lambda i: (i, 0))
    stat = pl.BlockSpec((tm, LANES), lambda i: (i, 0))
    return pl.pallas_call(
        _dil_out_kernel,
        out_shape=jax.ShapeDtypeStruct((t, n), F32),
        grid=(t // tm,),
        in_specs=[row] * 3 + [stat] * 6 + [
            pl.BlockSpec((LANES, n), lambda i: (0, 0)),
            pl.BlockSpec((n, n), lambda i: (0, 0)),
            row,
        ],
        out_specs=row,
        compiler_params=_cparams("arbitrary"),
        name="dil_out",
    )(*os_, *ms, *ls, expand, w, res)


def _cumsum_kernel(x_ref, c_ref, ct_ref, carry_sc, *, tb):
    @pl.when(pl.program_id(0) == 0)
    def _():
        carry_sc[...] = jnp.zeros(carry_sc.shape, F32)

    x = x_ref[...]
    row = lax.broadcasted_iota(jnp.int32, (tb, tb), 0)
    colm = lax.broadcasted_iota(jnp.int32, (tb, tb), 1)
    tri = jnp.where(colm <= row, 1.0, 0.0).astype(BF16)
    hi = x.astype(BF16)
    r1 = x - hi.astype(F32)
    mid = r1.astype(BF16)
    lo = (r1 - mid.astype(F32)).astype(BF16)
    c = carry_sc[...] + (_dot(tri, hi) + _dot(tri, mid) + _dot(tri, lo))
    c2 = c * LOG2E
    c_ref[...] = c2
    ct_ref[...] = c2.T
    carry_sc[...] = c[tb - 1:tb, :]


def _cumsum(x, tb=256):
    t = x.shape[0]
    tb = min(tb, t)
    return pl.pallas_call(
        functools.partial(_cumsum_kernel, tb=tb),
        out_shape=(jax.ShapeDtypeStruct((t, LANES), F32), jax.ShapeDtypeStruct((LANES, t), F32)),
        grid=(t // tb,),
        in_specs=[pl.BlockSpec((tb, LANES), lambda i: (i, 0))],
        out_specs=(pl.BlockSpec((tb, LANES), lambda i: (i, 0)), pl.BlockSpec((LANES, tb), lambda i: (0, i))),
        scratch_shapes=[pltpu.VMEM((1, LANES), F32)],
        compiler_params=_cparams("arbitrary"),
        name="fox_cumsum",
    )(x)


def _ffn_kernel(h_ref, hp_ref, g_ref, wg_ref, wv_ref, cwg_ref, cwv_ref, cbg_ref, cbv_ref, wo_ref, gf_ref, o_ref, a_sc,
                *, tm, final):
    i = pl.program_id(0)
    j = pl.program_id(1)

    @pl.when(j == 0)
    def _():
        g = g_ref[...]
        halo = _rms(hp_ref[...], g)
        a_sc[0:CONV_HALO, :] = jnp.where(i == 0, 0.0, halo).astype(BF16)
        a_sc[CONV_HALO:, :] = _rms(h_ref[...], g).astype(BF16)
        o_ref[...] = h_ref[...]

    a = a_sc[...]

    def conv(w_ref, cw_ref, cb_ref, lo, hi):
        u = _dot(a, w_ref[:, lo:hi])
        cw = cw_ref[:, lo:hi]
        c = cb_ref[:, lo:hi] + pltpu.roll(u, 2, 0)[CONV_HALO:] * cw[0:1]
        c = c + pltpu.roll(u, 1, 0)[CONV_HALO:] * cw[1:2]
        return c + u[CONV_HALO:] * cw[2:3]

    tf = wo_ref.shape[0]
    contrib = None
    for lo in range(0, tf, FFN_SLAB):
        hi = lo + FFN_SLAB
        gate = conv(wg_ref, cwg_ref, cbg_ref, lo, hi)
        val = conv(wv_ref, cwv_ref, cbv_ref, lo, hi)
        act = (gate / (1.0 + jnp.exp(-gate))) * val
        part = _dot(act.astype(BF16), wo_ref[lo:hi, :])
        contrib = part if contrib is None else contrib + part
    o_ref[...] += contrib

    if final:
        @pl.when(j == pl.num_programs(1) - 1)
        def _():
            o_ref[...] = _rms(o_ref[...], gf_ref[...])


def _ffn(h, gain, w_in, conv_w, conv_b, w_out, final_gain, final, tm=1024, tf=512):
    t, dm = h.shape
    ff = w_out.shape[0]
    tm = min(tm, t)
    nf = ff // tf
    halo_blocks = tm // CONV_HALO
    once = pl.Buffered(1)
    return pl.pallas_call(
        functools.partial(_ffn_kernel, tm=tm, final=final),
        out_shape=jax.ShapeDtypeStruct((t, dm), F32),
        grid=(t // tm, nf),
        in_specs=[
            pl.BlockSpec((tm, dm), lambda i, j: (i, 0), pipeline_mode=once),
            pl.BlockSpec((CONV_HALO, dm), lambda i, j: (jnp.maximum(i * halo_blocks - 1, 0), 0)),
            pl.BlockSpec((1, dm), lambda i, j: (0, 0)),
            pl.BlockSpec((dm, tf), lambda i, j: (0, j)),
            pl.BlockSpec((dm, tf), lambda i, j: (0, j + nf)),
            pl.BlockSpec((CONV_WIDTH, tf), lambda i, j: (0, j)),
            pl.BlockSpec((CONV_WIDTH, tf), lambda i, j: (0, j + nf)),
            pl.BlockSpec((1, tf), lambda i, j: (0, j)),
            pl.BlockSpec((1, tf), lambda i, j: (0, j + nf)),
            pl.BlockSpec((tf, dm), lambda i, j: (j, 0)),
            pl.BlockSpec((1, dm), lambda i, j: (0, 0)),
        ],
        out_specs=pl.BlockSpec((tm, dm), lambda i, j: (i, 0)),
        scratch_shapes=[pltpu.VMEM((tm + CONV_HALO, dm), BF16)],
        compiler_params=_cparams("arbitrary", "arbitrary"),
        name="conv_ffn",
    )(h, h, gain.reshape(1, dm), w_in, w_in, conv_w, conv_w, conv_b.reshape(1, -1), conv_b.reshape(1, -1), w_out,
      final_gain.reshape(1, dm))


def _mla(h, gain, positions, w_in, g_q, g_kv, w_qb, w_kvb, w_o, tm=1024):
    t, dm = h.shape
    tm = min(tm, t)
    hh = MLA_HEADS
    qk = MLA_NOPE + MLA_ROPE
    lat_w = MLA_Q_RANK + MLA_KV_RANK + LANES
    cos_t, sin_t = _rope_tables(positions)
    w_in_p = jnp.pad(w_in, ((0, 0), (0, lat_w - w_in.shape[1]))).astype(BF16)
    w_qb_p = jnp.pad(w_qb.reshape(MLA_Q_RANK, hh, qk), ((0, 0), (0, 0), (0, 2 * LANES - qk)))
    w_qb_p = w_qb_p.reshape(MLA_Q_RANK, hh * 2 * LANES).astype(BF16)

    row128 = pl.BlockSpec((tm, LANES), lambda i, j: (i, 0))

    def lat_epilogue(y, extras, outs):
        gq_ref, gkv_ref, c_ref, s_ref = extras
        cq_ref, ckv_ref, kr_ref = outs
        cq_ref[...] = _rms(y[:, :MLA_Q_RANK], gq_ref[...]).astype(BF16)
        ckv_ref[...] = _rms(y[:, MLA_Q_RANK:MLA_Q_RANK + MLA_KV_RANK], gkv_ref[...]).astype(BF16)
        kr_ref[...] = _rope128(y[:, MLA_Q_RANK + MLA_KV_RANK:], c_ref[...], s_ref[...]).astype(BF16)

    rank_row = pl.BlockSpec((tm, MLA_Q_RANK), lambda i, j: (i, 0))
    c_q, c_kv, k_rope = _proj(
        "mla_latents", h, gain, w_in_p,
        extras=[(g_q.reshape(1, -1), pl.BlockSpec((1, MLA_Q_RANK), lambda i, j: (0, 0))),
                (g_kv.reshape(1, -1), pl.BlockSpec((1, MLA_KV_RANK), lambda i, j: (0, 0))),
                (cos_t, row128), (sin_t, row128)],
        outs=[((t, MLA_Q_RANK), BF16, rank_row), ((t, MLA_KV_RANK), BF16, rank_row), ((t, LANES), BF16, row128)],
        epilogue=lat_epilogue, tm=tm, tn=lat_w)

    heads_per_tile = 4
    tn = heads_per_tile * 2 * LANES
    scale = qk ** -0.5 * LOG2E

    def q_epilogue(y, extras, outs):
        c_ref, s_ref = extras
        (q_ref,) = outs
        y = y * scale
        for a in range(heads_per_tile):
            base = a * 2 * LANES
            q_ref[a, :, :LANES] = y[:, base:base + LANES].astype(BF16)
            q_ref[a, :, LANES:] = _rope128(y[:, base + LANES:base + 2 * LANES], c_ref[...], s_ref[...]).astype(BF16)

    head_blk = pl.BlockSpec((heads_per_tile, tm, 2 * LANES), lambda i, j: (j, i, 0))
    (q_cat,) = _proj(
        "mla_q", c_q, None, w_qb_p,
        extras=[(cos_t, row128), (sin_t, row128)],
        outs=[((hh, t, 2 * LANES), BF16, head_blk)],
        epilogue=q_epilogue, tm=tm, tn=tn)

    def kv_epilogue(y, extras, outs):
        (kr_ref,) = extras
        k_ref, v_ref = outs
        for a in range(heads_per_tile):
            base = a * 2 * LANES
            k_ref[a, :, :LANES] = y[:, base:base + LANES].astype(BF16)
            k_ref[a, :, LANES:] = kr_ref[...]
            v_ref[a, :, :LANES] = y[:, base + LANES:base + 2 * LANES].astype(BF16)
            v_ref[a, :, LANES:] = jnp.ones((tm, LANES), BF16)

    k_cat, v = _proj(
        "mla_kv", c_kv, None, w_kvb.astype(BF16),
        extras=[(k_rope, row128)],
        outs=[((hh, t, 2 * LANES), BF16, head_blk), ((hh, t, 2 * LANES), BF16, head_blk)],
        epilogue=kv_epilogue, tm=tm, tn=tn)

    o = _flash("mla_attn", q_cat, k_cat, v, 0, 0, 0, hh)
    return _out_proj("mla_out", o, w_o.astype(BF16), jnp.zeros((1, dm), F32), h)


def _cols_epilogue(y, extras, outs):
    b_ref, sc_ref = extras
    (o_ref,) = outs
    o_ref[...] = ((y + b_ref[...]) * sc_ref[...]).astype(BF16)


def _swa(h, gain, rel_bias, w_qkv, b_qkv, sinks, w_o, b_o, tm=1024, tn=512):
    t, dm = h.shape
    tm = min(tm, t)
    n = w_qkv.shape[1]
    nq = SWA_Q_HEADS * SWA_HEAD_DIM
    grp = SWA_Q_HEADS // SWA_KV_HEADS
    colscale = jnp.concatenate([jnp.full((nq,), SWA_HEAD_DIM ** -0.5 * LOG2E, F32), jnp.ones((n - nq,), F32)])[None]
    vec = pl.BlockSpec((1, tn), lambda i, j: (0, j))
    (qkv,) = _proj(
        "swa_qkv", h, gain, w_qkv.astype(BF16),
        extras=[(b_qkv.reshape(1, n), vec), (colscale, vec)],
        outs=[((t, n), BF16, pl.BlockSpec((tm, tn), lambda i, j: (i, j)))],
        epilogue=_cols_epilogue, tm=tm, tn=tn)
    tab = _band_table(rel_bias, 1, SWA_WINDOW - 1)
    sink_rep = jnp.repeat(sinks.astype(F32) * LOG2E, SWA_HEAD_DIM)[None]
    k0 = nq // LANES
    v0 = k0 + SWA_KV_HEADS * SWA_HEAD_DIM // LANES
    (o,) = _band("swa_attn", qkv, lambda j: j, lambda j: k0 + j, lambda j: v0 + j, SWA_KV_HEADS // 2, 1, grp,
                 tab, lambda j: j, sink_rep=sink_rep, tile=BLK)
    return _out_proj("swa_out", o, w_o.astype(BF16), b_o.reshape(1, dm), h)


def _dilated(h, gain, rel_bias, w_qkv, w_o, tm=1024, tn=512):
    t, dm = h.shape
    tm = min(tm, t)
    n = w_qkv.shape[1]
    nq = DIL_HEADS * DIL_HEAD_DIM
    colscale = jnp.concatenate([jnp.full((nq,), DIL_HEAD_DIM ** -0.5 * LOG2E, F32), jnp.ones((n - nq,), F32)])[None]
    vec = pl.BlockSpec((1, tn), lambda i, j: (0, j))
    (qkv,) = _proj(
        "dil_qkv", h, gain, w_qkv.astype(BF16),
        extras=[(jnp.zeros((1, n), F32), vec), (colscale, vec)],
        outs=[((t, n), BF16, pl.BlockSpec((tm, tn), lambda i, j: (i, j)))],
        epilogue=_cols_epilogue, tm=tm, tn=tn)
    npairs = 2
    groups = DIL_HEADS // (2 * npairs)
    cols = n // (LANES * npairs)
    os_, ms, ls = [], [], []
    for window, rate in DIL_PAIRS:
        n_keys = window // rate
        tab = _band_table(rel_bias, rate, n_keys)
        x = qkv.reshape(t // rate, rate * n)
        o, m, l = _band(
            f"dil_attn_r{rate}", x,
            lambda j: (j // groups) * cols + j % groups,
            lambda j: (j // groups) * cols + groups + j % groups,
            lambda j: (j // groups) * cols + 2 * groups + j % groups,
            rate * groups, npairs, 1, tab, lambda j: j % groups)
        os_.append(o.reshape(t, nq))
        ms.append(m[:, :rate * DIL_HEADS].reshape(t, DIL_HEADS))
        ls.append(l[:, :rate * DIL_HEADS].reshape(t, DIL_HEADS))
    pad = lambda a: jnp.pad(a, ((0, 0), (0, LANES - DIL_HEADS)))
    ms = [pad(m) for m in ms]
    ls = [pad(l) for l in ls]
    expand = jnp.repeat(jnp.eye(LANES, DIL_HEADS, dtype=BF16), DIL_HEAD_DIM, axis=1)
    return _dil_out(os_, ms, ls, expand, w_o.astype(BF16), h)


def _fox(h, gain, w_in, b_f, w_o, tm=1024, tn=512):
    t, dm = h.shape
    tm = min(tm, t)
    hh = FOX_HEADS
    hd = hh * FOX_HEAD_DIM
    heads_per_tile = tn // FOX_HEAD_DIM
    colscale = jnp.concatenate([jnp.full((hd,), FOX_HEAD_DIM ** -0.5 * LOG2E, F32), jnp.ones((hd,), F32)])[None]
    w_all = jnp.pad(w_in, ((0, 0), (0, 3 * hd + LANES - w_in.shape[1]))).astype(BF16)

    def qk_epilogue(y, extras, outs):
        (sc_ref,) = extras
        (o_ref,) = outs
        y = y * sc_ref[...]
        for a in range(heads_per_tile):
            o_ref[a] = y[:, a * FOX_HEAD_DIM:(a + 1) * FOX_HEAD_DIM].astype(BF16)

    (qk,) = _proj(
        "fox_qk", h, gain, w_all,
        extras=[(colscale, pl.BlockSpec((1, tn), lambda i, j: (0, j)))],
        outs=[((2 * hh, t, FOX_HEAD_DIM), BF16,
               pl.BlockSpec((heads_per_tile, tm, FOX_HEAD_DIM), lambda i, j: (j, i, 0)))],
        epilogue=qk_epilogue, tm=tm, tn=tn, col0=0, n=2 * hd)

    def v_epilogue(y, extras, outs):
        (o_ref,) = outs
        for a in range(heads_per_tile):
            o_ref[a, :, :LANES] = y[:, a * FOX_HEAD_DIM:(a + 1) * FOX_HEAD_DIM].astype(BF16)
            o_ref[a, :, LANES:] = jnp.ones((tm, LANES), BF16)

    (v_aug,) = _proj(
        "fox_v", h, gain, w_all,
        extras=[],
        outs=[((hh, t, 2 * LANES), BF16, pl.BlockSpec((heads_per_tile, tm, 2 * LANES), lambda i, j: (j, i, 0)))],
        epilogue=v_epilogue, tm=tm, tn=tn, col0=2 * hd, n=hd)

    def gate_epilogue(y, extras, outs):
        (b_ref,) = extras
        (o_ref,) = outs
        x = y + b_ref[...]
        o_ref[...] = jnp.minimum(x, 0.0) - jnp.log(1.0 + jnp.exp(-jnp.abs(x)))

    b_gate = jnp.pad(b_f.astype(F32), (0, LANES - hh))[None]
    (log_f,) = _proj(
        "fox_gate", h, gain, w_all,
        extras=[(b_gate, pl.BlockSpec((1, LANES), lambda i, j: (0, 0)))],
        outs=[((t, LANES), F32, pl.BlockSpec((tm, LANES), lambda i, j: (i, 0)))],
        epilogue=gate_epilogue, tm=tm, tn=LANES, col0=3 * hd, n=LANES)
    cum, cum_t = _cumsum(log_f)
    o = _flash("fox_attn", qk, qk, v_aug, 0, hh, 0, hh, cum=cum, cumt=cum_t)
    return _out_proj("fox_out", o, w_o.astype(BF16), jnp.zeros((1, dm), F32), h)


def kernel(x, positions, rel_bias, norm_mix, norm_ffn, mla_w_in, mla_g_q, mla_g_kv, mla_w_qb, mla_w_kvb, mla_w_o, swa_w_qkv, swa_b_qkv, swa_sinks, swa_w_o, swa_b_o, dil_w_qkv, dil_w_o, fox_w_in, fox_b_f, fox_w_o, ffn_w_in, ffn_conv_w, ffn_conv_b, ffn_w_out, final_norm):
    bsz, t, dm = x.shape
    assert bsz == 1
    depth = norm_mix.shape[0]
    h = x.reshape(t, dm)
    pos = positions.reshape(t)
    for i in range(depth):
        kind = i % 4
        r = i // 4
        if kind == 0:
            h = _mla(h, norm_mix[i], pos, mla_w_in[r], mla_g_q[r], mla_g_kv[r], mla_w_qb[r], mla_w_kvb[r], mla_w_o[r])
        elif kind == 1:
            h = _swa(h, norm_mix[i], rel_bias, swa_w_qkv[r], swa_b_qkv[r], swa_sinks[r], swa_w_o[r], swa_b_o[r])
        elif kind == 2:
            h = _dilated(h, norm_mix[i], rel_bias, dil_w_qkv[r], dil_w_o[r])
        else:
            h = _fox(h, norm_mix[i], fox_w_in[r], fox_b_f[r], fox_w_o[r])
        h = _ffn(h, norm_ffn[i], ffn_w_in[i].astype(BF16), ffn_conv_w[i], ffn_conv_b[i], ffn_w_out[i].astype(BF16),
                 final_norm, final=(i == depth - 1))
    return h.reshape(bsz, t, dm)
```

```python
import functools
import math

import jax
import jax.numpy as jnp
from jax import lax
from jax.experimental import pallas as pl
from jax.experimental.pallas import tpu as pltpu

F32 = jnp.float32
BF16 = jnp.bfloat16
EPS = 1e-6
NEG = -1e30
LOG2E = math.log2(math.e)

LANES = 128
VMEM_LIMIT_BYTES = 56 * 1024 * 1024
BLK = 128

ROPE_THETA = 10000.0
REL_BUCKETS = 32
REL_MAX_DIST = 2048
MLA_HEADS, MLA_Q_RANK, MLA_KV_RANK, MLA_NOPE, MLA_ROPE, MLA_V = 16, 512, 512, 128, 64, 128
SWA_Q_HEADS, SWA_KV_HEADS, SWA_HEAD_DIM, SWA_WINDOW = 32, 4, 64, 128
DIL_HEADS, DIL_HEAD_DIM = 32, 64
DIL_PAIRS = ((128, 1), (512, 4), (2048, 16))
FOX_HEADS, FOX_HEAD_DIM = 16, 128
CONV_WIDTH = 3
CONV_HALO = 16
FFN_SLAB = 512


def _cparams(*sem):
    return pltpu.CompilerParams(dimension_semantics=sem, vmem_limit_bytes=VMEM_LIMIT_BYTES)


def _rms(x, g):
    return x * lax.rsqrt(jnp.mean(x * x, axis=-1, keepdims=True) + EPS) * g


def _dot(a, b):
    return jnp.dot(a, b, preferred_element_type=F32)


def _dot_nt(a, b):
    return lax.dot_general(a, b, (((1,), (1,)), ((), ())), preferred_element_type=F32)


def _rope_tab_kernel(pos_ref, f_ref, sgn_ref, c_ref, s_ref):
    ang = pos_ref[...].astype(F32) * f_ref[...]
    c_ref[...] = jnp.cos(ang)
    s_ref[...] = jnp.sin(ang) * sgn_ref[...]


def _rope_tables(positions):
    t = positions.shape[0]
    half = MLA_ROPE // 2
    inv_freq = ROPE_THETA ** (-jnp.arange(half, dtype=F32) / half)
    zeros = jnp.zeros((LANES - 2 * half,), F32)
    freq = jnp.concatenate([inv_freq, inv_freq, zeros])[None]
    sgn = jnp.concatenate([-jnp.ones((half,), F32), jnp.ones((half,), F32), zeros])[None]
    tm = min(t, 1024)
    row = pl.BlockSpec((tm, LANES), lambda i: (i, 0))
    vec = pl.BlockSpec((1, LANES), lambda i: (0, 0))
    return pl.pallas_call(
        _rope_tab_kernel,
        out_shape=(jax.ShapeDtypeStruct((t, LANES), F32),) * 2,
        grid=(t // tm,),
        in_specs=[pl.BlockSpec((tm, 1), lambda i: (i, 0)), vec, vec],
        out_specs=(row, row),
        compiler_params=_cparams("arbitrary"),
        name="rope_tables",
    )(positions.reshape(t, 1), freq, sgn)


def _rope128(z, c, s):
    half = MLA_ROPE // 2
    lane = lax.broadcasted_iota(jnp.int32, z.shape, 1)
    swapped = jnp.where(lane < half, pltpu.roll(z, LANES - half, 1), pltpu.roll(z, half, 1))
    return z * c + swapped * s


def _proj_kernel(*refs, norm, n_extra, epilogue):
    if norm:
        lhs_ref, g_ref, w_ref = refs[:3]
        rest = refs[3:]
        a_ref = rest[-1]
        rest = rest[:-1]

        @pl.when(pl.program_id(1) == 0)
        def _():
            a_ref[...] = _rms(lhs_ref[...], g_ref[...]).astype(BF16)

        a = a_ref[...]
    else:
        lhs_ref, w_ref = refs[:2]
        rest = refs[2:]
        a = lhs_ref[...]
    y = _dot(a, w_ref[...])
    epilogue(y, rest[:n_extra], rest[n_extra:])


def _proj(name, lhs, gain, w, extras, outs, epilogue, tm, tn, col0=0, n=None, scratch=()):
    m, k = lhs.shape
    n = w.shape[1] if n is None else n
    woff = col0 // tn
    norm = gain is not None
    in_specs = [pl.BlockSpec((tm, k), lambda i, j: (i, 0))]
    args = [lhs]
    if norm:
        in_specs.append(pl.BlockSpec((1, k), lambda i, j: (0, 0)))
        args.append(gain.reshape(1, k))
    in_specs.append(pl.BlockSpec((k, tn), lambda i, j: (0, woff + j)))
    args.append(w)
    for arr, spec in extras:
        in_specs.append(spec)
        args.append(arr)
    res = pl.pallas_call(
        functools.partial(_proj_kernel, norm=norm, n_extra=len(extras), epilogue=epilogue),
        out_shape=tuple(jax.ShapeDtypeStruct(s, d) for s, d, _ in outs),
        grid=(m // tm, n // tn),
        in_specs=in_specs,
        out_specs=tuple(spec for _, _, spec in outs),
        scratch_shapes=list(scratch) + ([pltpu.VMEM((tm, k), BF16)] if norm else []),
        compiler_params=_cparams("arbitrary", "arbitrary"),
        name=name,
    )(*args)
    return res


def _out_proj_kernel(lhs_ref, w_ref, b_ref, res_ref, o_ref):
    o_ref[...] = res_ref[...] + (_dot(lhs_ref[...], w_ref[...]) + b_ref[...])


def _out_proj(name, lhs, w, bias, res, tm=512):
    m, k = lhs.shape
    n = w.shape[1]
    tm = min(tm, m)
    return pl.pallas_call(
        _out_proj_kernel,
        out_shape=jax.ShapeDtypeStruct((m, n), F32),
        grid=(m // tm,),
        in_specs=[
            pl.BlockSpec((tm, k), lambda i: (i, 0)),
            pl.BlockSpec((k, n), lambda i: (0, 0)),
            pl.BlockSpec((1, n), lambda i: (0, 0)),
            pl.BlockSpec((tm, n), lambda i: (i, 0)),
        ],
        out_specs=pl.BlockSpec((tm, n), lambda i: (i, 0)),
        compiler_params=_cparams("arbitrary"),
        name=name,
    )(lhs, w, bias, res)


def _flash_kernel(*refs, bq, fox):
    if fox:
        q_ref, k_ref, v_ref, cum_ref, cumt_ref, o_ref, s0_sc, s1_sc, m_sc, acc_sc = refs
    else:
        q_ref, k_ref, v_ref, o_ref, s0_sc, s1_sc, m_sc, acc_sc = refs
    h = pl.program_id(0)
    i = pl.program_id(1)
    reps = bq // LANES
    m_sc[...] = jnp.full(m_sc.shape, NEG, F32)
    acc_sc[...] = jnp.zeros(acc_sc.shape, F32)
    q = q_ref[0]
    if fox:
        lane = lax.broadcasted_iota(jnp.int32, (bq, LANES), 1)
        cq = jnp.sum(jnp.where(lane == h, cum_ref[...], 0.0), axis=1, keepdims=True)
        cq = jnp.broadcast_to(cq, (bq, LANES))

    def scores(c, dst):
        k0 = pl.multiple_of(c * bq, bq)
        dst[...] = _dot_nt(q, k_ref[0, pl.ds(k0, bq), :])

    def softmax_pv(c, src, masked):
        k0 = pl.multiple_of(c * bq, bq)
        t = src[...]
        if fox:
            t = t - cumt_ref[pl.ds(h, 1), pl.ds(k0, bq)]
        if masked:
            row = lax.broadcasted_iota(jnp.int32, (bq, bq), 0)
            col = lax.broadcasted_iota(jnp.int32, (bq, bq), 1)
            t = jnp.where(col <= row, t, NEG)
        r = jnp.max(t, axis=1, keepdims=True)
        m_prev = m_sc[...]
        m_new = jnp.maximum(m_prev, r + cq) if fox else jnp.maximum(m_prev, r)
        alpha = jnp.exp2(m_prev - m_new)
        shift = m_new - cq if fox else m_new
        p = jnp.exp2(t - jnp.tile(shift, (1, reps)))
        pv = _dot(p.astype(BF16), v_ref[0, pl.ds(k0, bq), :])
        acc_sc[...] = jnp.tile(alpha, (1, 2)) * acc_sc[...] + pv
        m_sc[...] = m_new

    scores(0, s0_sc)

    def body(u, carry):
        c = 2 * u
        scores(c + 1, s1_sc)
        softmax_pv(c, s0_sc, False)
        scores(c + 2, s0_sc)
        softmax_pv(c + 1, s1_sc, False)
        return carry

    lax.fori_loop(0, lax.shift_right_logical(i, 1), body, 0)

    @pl.when(lax.bitwise_and(i, 1) == 1)
    def _():
        scores(i, s1_sc)
        softmax_pv(i - 1, s0_sc, False)
        softmax_pv(i, s1_sc, True)

    @pl.when(lax.bitwise_and(i, 1) == 0)
    def _():
        softmax_pv(i, s0_sc, True)

    acc = acc_sc[...]
    o_ref[...] = (acc[:, :LANES] / acc[:, LANES:]).astype(BF16)


def _flash(name, q, k, v, q_off, k_off, v_off, heads, cum=None, cumt=None, bq=512):
    t = q.shape[1]
    dk = q.shape[2]
    dv = v.shape[2]
    assert dv == 2 * LANES
    bq = min(bq, t)
    fox = cum is not None
    in_specs = [
        pl.BlockSpec((1, bq, dk), lambda h, i: (q_off + h, i, 0)),
        pl.BlockSpec((1, t, dk), lambda h, i: (k_off + h, 0, 0)),
        pl.BlockSpec((1, t, dv), lambda h, i: (v_off + h, 0, 0)),
    ]
    args = [q, k, v]
    if fox:
        in_specs += [
            pl.BlockSpec((bq, LANES), lambda h, i: (i, 0)),
            pl.BlockSpec(cumt.shape, lambda h, i: (0, 0)),
        ]
        args += [cum, cumt]
    return pl.pallas_call(
        functools.partial(_flash_kernel, bq=bq, fox=fox),
        out_shape=jax.ShapeDtypeStruct((t, heads * LANES), BF16),
        grid=(heads, t // bq),
        in_specs=in_specs,
        out_specs=pl.BlockSpec((bq, LANES), lambda h, i: (i, h)),
        scratch_shapes=[pltpu.VMEM((bq, bq), F32), pltpu.VMEM((bq, bq), F32),
                        pltpu.VMEM((bq, LANES), F32), pltpu.VMEM((bq, dv), F32)],
        compiler_params=_cparams("arbitrary", "arbitrary"),
        name=name,
    )(*args)


def _band_kernel(*refs, nbk, npairs, grp, sinks, tiles_per_seq):
    if sinks:
        q_ref, k_ref, v_ref, kp_ref, vp_ref, tab_ref, sink_ref, o_ref, kf_sc, vf_sc = refs
    else:
        q_ref, k_ref, v_ref, kp_ref, vp_ref, tab_ref, o_ref, m_ref, l_ref, kf_sc, vf_sc = refs
    i = pl.program_id(0)
    j = pl.program_id(1)
    kf_sc[0:BLK, :] = kp_ref[...]
    kf_sc[BLK:, :] = k_ref[...]
    vf_sc[0:BLK, :] = vp_ref[...]
    vf_sc[BLK:, :] = v_ref[...]
    d = LANES // 2
    n_kv = 2 * npairs
    col = lax.broadcasted_iota(jnp.int32, (1, 2 * BLK), 1)
    penrow = jnp.where(jnp.logical_and(col < BLK, i % tiles_per_seq == 0), NEG, 0.0)
    if not sinks:
        stat_lane = lax.broadcasted_iota(jnp.int32, (BLK, LANES), 1)
        steps_per_stat = LANES // n_kv

        @pl.when(j % steps_per_stat == 0)
        def _():
            m_ref[...] = jnp.zeros(m_ref.shape, F32)
            l_ref[...] = jnp.zeros(l_ref.shape, F32)

    lo_k = lax.broadcasted_iota(jnp.int32, (2 * BLK, LANES), 1) < d
    lo_o = lax.broadcasted_iota(jnp.int32, (BLK, LANES), 1) < d
    swap = lambda a: jnp.concatenate([a[:, d:], a[:, :d]], axis=1)

    for b in range(nbk):
        r0 = b * BLK
        if not sinks:
            m_st = m_ref[r0:r0 + BLK, :]
            l_st = l_ref[r0:r0 + BLK, :]
        for kp in range(npairs):
            kk = kf_sc[r0:r0 + 2 * BLK, kp * LANES:(kp + 1) * LANES]
            vv = vf_sc[r0:r0 + 2 * BLK, kp * LANES:(kp + 1) * LANES]
            zero = jnp.zeros_like(kk)
            k_own = [jnp.where(lo_k, kk, zero), jnp.where(lo_k, zero, kk)]
            if grp > 1:
                k_by = [[k_own[0], swap(k_own[0])], [swap(k_own[1]), k_own[1]]]
                v_by = [[vv, swap(vv)], [swap(vv), vv]]
            else:
                k_by = [[k_own[0], None], [None, k_own[1]]]
                v_by = [[vv, None], [None, vv]]
            for c in range(grp):
                qcol = kp * grp + c
                qq = q_ref[r0:r0 + BLK, qcol * LANES:(qcol + 1) * LANES]
                halves = []
                for qh in range(2):
                    hq = 2 * qcol + qh
                    kh = hq // grp - 2 * kp
                    s = _dot_nt(qq, k_by[kh][qh]) + tab_ref[hq]
                    if b == 0:
                        s = s + penrow
                    m = jnp.max(s, axis=1, keepdims=True)
                    p = jnp.exp2(s - m)
                    l = jnp.sum(p, axis=1, keepdims=True)
                    acc = _dot(p.astype(BF16), v_by[kh][qh])
                    if sinks:
                        sink = sink_ref[:, qcol * LANES:(qcol + 1) * LANES]
                        m2 = jnp.maximum(m, sink)
                        a = jnp.exp2(m - m2)
                        den = l * a + jnp.exp2(sink - m2)
                        halves.append(acc * (a / den))
                    else:
                        halves.append(acc / l)
                        sel = stat_lane == (n_kv * j + hq) % LANES
                        m_st = jnp.where(sel, m, m_st)
                        l_st = jnp.where(sel, l, l_st)
                o_ref[r0:r0 + BLK, qcol * LANES:(qcol + 1) * LANES] = jnp.where(lo_o, halves[0], halves[1]).astype(BF16)
        if not sinks:
            m_ref[r0:r0 + BLK, :] = m_st
            l_ref[r0:r0 + BLK, :] = l_st


def _band(name, x, q_blk, k_blk, v_blk, n_steps, npairs, grp, tab, tab_blk, sink_rep=None, tile=512, seq_len=None):
    length = x.shape[0]
    seq_len = length if seq_len is None else seq_len
    tb = min(tile, seq_len)
    nbk = tb // BLK
    kw = LANES * npairs
    qw = kw * grp
    n_kv = 2 * npairs
    n_out = n_steps * qw
    sinks = sink_rep is not None
    prev = lambda i, j: jnp.maximum(i * nbk - 1, 0)
    in_specs = [
        pl.BlockSpec((tb, qw), lambda i, j: (i, q_blk(j))),
        pl.BlockSpec((tb, kw), lambda i, j: (i, k_blk(j))),
        pl.BlockSpec((tb, kw), lambda i, j: (i, v_blk(j))),
        pl.BlockSpec((BLK, kw), lambda i, j: (prev(i, j), k_blk(j))),
        pl.BlockSpec((BLK, kw), lambda i, j: (prev(i, j), v_blk(j))),
        pl.BlockSpec((n_kv * grp, BLK, 2 * BLK), lambda i, j: (tab_blk(j), 0, 0)),
    ]
    args = [x, x, x, x, x, tab]
    out_shape = [jax.ShapeDtypeStruct((length, n_out), BF16)]
    out_specs = [pl.BlockSpec((tb, qw), lambda i, j: (i, j))]
    if sinks:
        in_specs.append(pl.BlockSpec((1, qw), lambda i, j: (0, j)))
        args.append(sink_rep)
    else:
        n_stat = max(LANES, n_kv * n_steps)
        stat = pl.BlockSpec((tb, LANES), lambda i, j: (i, j // (LANES // n_kv)))
        out_shape += [jax.ShapeDtypeStruct((length, n_stat), F32)] * 2
        out_specs += [stat, stat]
    return pl.pallas_call(
        functools.partial(_band_kernel, nbk=nbk, npairs=npairs, grp=grp, sinks=sinks, tiles_per_seq=seq_len // tb),
        out_shape=tuple(out_shape),
        grid=(length // tb, n_steps),
        in_specs=in_specs,
        out_specs=tuple(out_specs),
        scratch_shapes=[pltpu.VMEM((tb + BLK, kw), BF16), pltpu.VMEM((tb + BLK, kw), BF16)],
        compiler_params=_cparams("arbitrary", "arbitrary"),
        name=name,
    )(*args)


def _t5_bucket(n):
    exact = REL_BUCKETS // 2
    nf = jnp.maximum(n, 1).astype(F32)
    large = exact + (jnp.log(nf / exact) / math.log(REL_MAX_DIST / exact) * (REL_BUCKETS - exact)).astype(jnp.int32)
    return jnp.where(n < exact, n, jnp.minimum(large, REL_BUCKETS - 1))


def _band_table(rel_bias, rate, max_dist):
    nh = rel_bias.shape[1]
    w = 2 * BLK
    dist = jnp.arange(w)
    row = rel_bias[_t5_bucket(rate * jnp.minimum(dist, max_dist))].T * LOG2E
    f = jnp.where(dist[None] <= max_dist, row, NEG).astype(F32)
    r = jnp.roll(f[:, ::-1], BLK + 1, axis=1)
    rr = jnp.concatenate([r, r], axis=1)
    flat = jnp.broadcast_to(rr[:, None, :], (nh, BLK + 1, 2 * w)).reshape(nh, (BLK + 1) * 2 * w)
    return flat[:, w:w + BLK * (2 * w - 1)].reshape(nh, BLK, 2 * w - 1)[:, :, :w]


def _dil_out_kernel(o1, o2, o3, m1, m2, m3, l1, l2, l3, e_ref, w_ref, res_ref, out_ref, row_sc, stat_sc, *, tm, rates):
    def in_order(ref, rate, sc):
        if rate == 1:
            return ref[0].astype(F32)
        groups = ref.shape[2] // LANES
        for c in range(rate):
            rows = ref[c].astype(F32)
            for k in range(groups):
                sc[k, pl.ds(c, tm // rate, stride=rate), :] = rows[:, k * LANES:(k + 1) * LANES]
        return jnp.concatenate([sc[k] for k in range(groups)], axis=1)

    ms = [in_order(m, r, stat_sc) for m, r in zip((m1, m2, m3), rates)]
    ls = [in_order(l, r, stat_sc) for l, r in zip((l1, l2, l3), rates)]
    lane = lax.broadcasted_iota(jnp.int32, ms[0].shape, 1)
    mx = jnp.maximum(jnp.maximum(ms[0], ms[1]), ms[2])
    ws = [l * jnp.exp2(m - mx) for m, l in zip(ms, ls)]
    den = ws[0] + ws[1] + ws[2]
    e = e_ref[...]
    o = None
    for w, o_ref, r in zip(ws, (o1, o2, o3), rates):
        wn = jnp.where(lane < DIL_HEADS, w / den, 0.0)
        hi = wn.astype(BF16)
        lo = (wn - hi.astype(F32)).astype(BF16)
        term = in_order(o_ref, r, row_sc) * (_dot(hi, e) + _dot(lo, e))
        o = term if o is None else o + term
    out_ref[...] = res_ref[...] + _dot(o.astype(BF16), w_ref[...])


def _dil_out(os_, ms, ls, rates, expand, w, res, tm=512):
    t, n = res.shape
    tm = min(tm, t)
    row = pl.BlockSpec((tm, n), lambda i: (i, 0))

    def by_residue(a, rate):
        width = a.shape[1]
        return a.reshape(rate, t // rate, width), pl.BlockSpec((rate, tm // rate, width), lambda i: (0, i, 0))

    branch = [by_residue(a, r) for group in (os_, ms, ls) for a, r in zip(group, rates)]
    return pl.pallas_call(
        functools.partial(_dil_out_kernel, tm=tm, rates=tuple(rates)),
        out_shape=jax.ShapeDtypeStruct((t, n), F32),
        grid=(t // tm,),
        in_specs=[spec for _, spec in branch] + [
            pl.BlockSpec((LANES, n), lambda i: (0, 0)),
            pl.BlockSpec((n, n), lambda i: (0, 0)),
            row,
        ],
        out_specs=row,
        scratch_shapes=[pltpu.VMEM((n // LANES, tm, LANES), F32), pltpu.VMEM((1, tm, LANES), F32)],
        compiler_params=_cparams("arbitrary"),
        name="dil_out",
    )(*[a for a, _ in branch], expand, w, res)


def _cumsum_kernel(x_ref, c_ref, ct_ref, carry_sc, *, tb):
    @pl.when(pl.program_id(0) == 0)
    def _():
        carry_sc[...] = jnp.zeros(carry_sc.shape, F32)

    x = x_ref[...]
    row = lax.broadcasted_iota(jnp.int32, (tb, tb), 0)
    colm = lax.broadcasted_iota(jnp.int32, (tb, tb), 1)
    tri = jnp.where(colm <= row, 1.0, 0.0).astype(BF16)
    hi = x.astype(BF16)
    r1 = x - hi.astype(F32)
    mid = r1.astype(BF16)
    lo = (r1 - mid.astype(F32)).astype(BF16)
    c = carry_sc[...] + (_dot(tri, hi) + _dot(tri, mid) + _dot(tri, lo))
    c2 = c * LOG2E
    c_ref[...] = c2
    ct_ref[...] = c2.T
    carry_sc[...] = c[tb - 1:tb, :]


def _cumsum(x, tb=256):
    t = x.shape[0]
    tb = min(tb, t)
    return pl.pallas_call(
        functools.partial(_cumsum_kernel, tb=tb),
        out_shape=(jax.ShapeDtypeStruct((t, LANES), F32), jax.ShapeDtypeStruct((LANES, t), F32)),
        grid=(t // tb,),
        in_specs=[pl.BlockSpec((tb, LANES), lambda i: (i, 0))],
        out_specs=(pl.BlockSpec((tb, LANES), lambda i: (i, 0)), pl.BlockSpec((LANES, tb), lambda i: (0, i))),
        scratch_shapes=[pltpu.VMEM((1, LANES), F32)],
        compiler_params=_cparams("arbitrary"),
        name="fox_cumsum",
    )(x)


def _ffn_kernel(h_ref, hp_ref, g_ref, wg_ref, wv_ref, cwg_ref, cwv_ref, cbg_ref, cbv_ref, wo_ref, gf_ref, o_ref, a_sc,
                *, tm, final):
    i = pl.program_id(0)
    j = pl.program_id(1)

    @pl.when(j == 0)
    def _():
        g = g_ref[...]
        halo = _rms(hp_ref[...], g)
        a_sc[0:CONV_HALO, :] = jnp.where(i == 0, 0.0, halo).astype(BF16)
        a_sc[CONV_HALO:, :] = _rms(h_ref[...], g).astype(BF16)
        o_ref[...] = h_ref[...]

    a = a_sc[...]

    def conv(w_ref, cw_ref, cb_ref, lo, hi):
        u = _dot(a, w_ref[:, lo:hi])
        cw = cw_ref[:, lo:hi]
        c = cb_ref[:, lo:hi] + pltpu.roll(u, 2, 0)[CONV_HALO:] * cw[0:1]
        c = c + pltpu.roll(u, 1, 0)[CONV_HALO:] * cw[1:2]
        return c + u[CONV_HALO:] * cw[2:3]

    tf = wo_ref.shape[0]
    contrib = None
    for lo in range(0, tf, FFN_SLAB):
        hi = lo + FFN_SLAB
        gate = conv(wg_ref, cwg_ref, cbg_ref, lo, hi)
        val = conv(wv_ref, cwv_ref, cbv_ref, lo, hi)
        act = (gate / (1.0 + jnp.exp(-gate))) * val
        part = _dot(act.astype(BF16), wo_ref[lo:hi, :])
        contrib = part if contrib is None else contrib + part
    o_ref[...] += contrib

    if final:
        @pl.when(j == pl.num_programs(1) - 1)
        def _():
            o_ref[...] = _rms(o_ref[...], gf_ref[...])


def _ffn(h, gain, layer, w_in, conv_w, conv_b, w_out, final_gain, final, tm=1024, tf=512):
    t, dm = h.shape
    ff = w_out.shape[1]
    tm = min(tm, t)
    nf = ff // tf
    halo_blocks = tm // CONV_HALO
    once = pl.Buffered(1)
    return pl.pallas_call(
        functools.partial(_ffn_kernel, tm=tm, final=final),
        out_shape=jax.ShapeDtypeStruct((t, dm), F32),
        grid=(t // tm, nf),
        in_specs=[
            pl.BlockSpec((tm, dm), lambda i, j: (i, 0), pipeline_mode=once),
            pl.BlockSpec((CONV_HALO, dm), lambda i, j: (jnp.maximum(i * halo_blocks - 1, 0), 0)),
            pl.BlockSpec((1, dm), lambda i, j: (0, 0)),
            pl.BlockSpec((None, dm, tf), lambda i, j: (layer, 0, j)),
            pl.BlockSpec((None, dm, tf), lambda i, j: (layer, 0, j + nf)),
            pl.BlockSpec((None, CONV_WIDTH, tf), lambda i, j: (layer, 0, j)),
            pl.BlockSpec((None, CONV_WIDTH, tf), lambda i, j: (layer, 0, j + nf)),
            pl.BlockSpec((None, 1, tf), lambda i, j: (layer, 0, j)),
            pl.BlockSpec((None, 1, tf), lambda i, j: (layer, 0, j + nf)),
            pl.BlockSpec((None, tf, dm), lambda i, j: (layer, j, 0)),
            pl.BlockSpec((1, dm), lambda i, j: (0, 0)),
        ],
        out_specs=pl.BlockSpec((tm, dm), lambda i, j: (i, 0)),
        scratch_shapes=[pltpu.VMEM((tm + CONV_HALO, dm), BF16)],
        compiler_params=_cparams("arbitrary", "arbitrary"),
        name="conv_ffn",
    )(h, h, gain.reshape(1, dm), w_in, w_in, conv_w, conv_w, conv_b[:, None, :], conv_b[:, None, :], w_out,
      final_gain.reshape(1, dm))


def _mla(h, gain, positions, w_in, g_q, g_kv, w_qb, w_kvb, w_o, tm=1024):
    t, dm = h.shape
    tm = min(tm, t)
    hh = MLA_HEADS
    qk = MLA_NOPE + MLA_ROPE
    lat_w = MLA_Q_RANK + MLA_KV_RANK + LANES
    cos_t, sin_t = _rope_tables(positions)
    w_in_p = jnp.pad(w_in, ((0, 0), (0, lat_w - w_in.shape[1]))).astype(BF16)
    w_qb_p = jnp.pad(w_qb.reshape(MLA_Q_RANK, hh, qk), ((0, 0), (0, 0), (0, 2 * LANES - qk)))
    w_qb_p = w_qb_p.reshape(MLA_Q_RANK, hh * 2 * LANES).astype(BF16)

    row128 = pl.BlockSpec((tm, LANES), lambda i, j: (i, 0))

    def lat_epilogue(y, extras, outs):
        gq_ref, gkv_ref, c_ref, s_ref = extras
        cq_ref, ckv_ref, kr_ref = outs
        cq_ref[...] = _rms(y[:, :MLA_Q_RANK], gq_ref[...]).astype(BF16)
        ckv_ref[...] = _rms(y[:, MLA_Q_RANK:MLA_Q_RANK + MLA_KV_RANK], gkv_ref[...]).astype(BF16)
        kr_ref[...] = _rope128(y[:, MLA_Q_RANK + MLA_KV_RANK:], c_ref[...], s_ref[...]).astype(BF16)

    rank_row = pl.BlockSpec((tm, MLA_Q_RANK), lambda i, j: (i, 0))
    c_q, c_kv, k_rope = _proj(
        "mla_latents", h, gain, w_in_p,
        extras=[(g_q.reshape(1, -1), pl.BlockSpec((1, MLA_Q_RANK), lambda i, j: (0, 0))),
                (g_kv.reshape(1, -1), pl.BlockSpec((1, MLA_KV_RANK), lambda i, j: (0, 0))),
                (cos_t, row128), (sin_t, row128)],
        outs=[((t, MLA_Q_RANK), BF16, rank_row), ((t, MLA_KV_RANK), BF16, rank_row), ((t, LANES), BF16, row128)],
        epilogue=lat_epilogue, tm=tm, tn=lat_w)

    heads_per_tile = 4
    tn = heads_per_tile * 2 * LANES
    scale = qk ** -0.5 * LOG2E

    def q_epilogue(y, extras, outs):
        c_ref, s_ref = extras
        (q_ref,) = outs
        y = y * scale
        for a in range(heads_per_tile):
            base = a * 2 * LANES
            q_ref[a, :, :LANES] = y[:, base:base + LANES].astype(BF16)
            q_ref[a, :, LANES:] = _rope128(y[:, base + LANES:base + 2 * LANES], c_ref[...], s_ref[...]).astype(BF16)

    head_blk = pl.BlockSpec((heads_per_tile, tm, 2 * LANES), lambda i, j: (j, i, 0))
    (q_cat,) = _proj(
        "mla_q", c_q, None, w_qb_p,
        extras=[(cos_t, row128), (sin_t, row128)],
        outs=[((hh, t, 2 * LANES), BF16, head_blk)],
        epilogue=q_epilogue, tm=tm, tn=tn)

    def kv_epilogue(y, extras, outs):
        (kr_ref,) = extras
        k_ref, v_ref = outs
        for a in range(heads_per_tile):
            base = a * 2 * LANES
            k_ref[a, :, :LANES] = y[:, base:base + LANES].astype(BF16)
            k_ref[a, :, LANES:] = kr_ref[...]
            v_ref[a, :, :LANES] = y[:, base + LANES:base + 2 * LANES].astype(BF16)
            v_ref[a, :, LANES:] = jnp.ones((tm, LANES), BF16)

    k_cat, v = _proj(
        "mla_kv", c_kv, None, w_kvb.astype(BF16),
        extras=[(k_rope, row128)],
        outs=[((hh, t, 2 * LANES), BF16, head_blk), ((hh, t, 2 * LANES), BF16, head_blk)],
        epilogue=kv_epilogue, tm=tm, tn=tn)

    o = _flash("mla_attn", q_cat, k_cat, v, 0, 0, 0, hh)
    return _out_proj("mla_out", o, w_o.astype(BF16), jnp.zeros((1, dm), F32), h)


def _cols_epilogue(y, extras, outs):
    b_ref, sc_ref = extras
    (o_ref,) = outs
    o_ref[...] = ((y + b_ref[...]) * sc_ref[...]).astype(BF16)


def _swa(h, gain, rel_bias, w_qkv, b_qkv, sinks, w_o, b_o, tm=1024, tn=512):
    t, dm = h.shape
    tm = min(tm, t)
    n = w_qkv.shape[1]
    nq = SWA_Q_HEADS * SWA_HEAD_DIM
    grp = SWA_Q_HEADS // SWA_KV_HEADS
    colscale = jnp.concatenate([jnp.full((nq,), SWA_HEAD_DIM ** -0.5 * LOG2E, F32), jnp.ones((n - nq,), F32)])[None]
    vec = pl.BlockSpec((1, tn), lambda i, j: (0, j))
    (qkv,) = _proj(
        "swa_qkv", h, gain, w_qkv.astype(BF16),
        extras=[(b_qkv.reshape(1, n), vec), (colscale, vec)],
        outs=[((t, n), BF16, pl.BlockSpec((tm, tn), lambda i, j: (i, j)))],
        epilogue=_cols_epilogue, tm=tm, tn=tn)
    tab = _band_table(rel_bias, 1, SWA_WINDOW - 1)
    sink_rep = jnp.repeat(sinks.astype(F32) * LOG2E, SWA_HEAD_DIM)[None]
    k0 = nq // LANES
    v0 = k0 + SWA_KV_HEADS * SWA_HEAD_DIM // LANES
    (o,) = _band("swa_attn", qkv, lambda j: j, lambda j: k0 + j, lambda j: v0 + j, SWA_KV_HEADS // 2, 1, grp,
                 tab, lambda j: j, sink_rep=sink_rep, tile=BLK)
    return _out_proj("swa_out", o, w_o.astype(BF16), b_o.reshape(1, dm), h)


def _dilated(h, gain, rel_bias, w_qkv, w_o, tm=1024, tn=512):
    t, dm = h.shape
    tm = min(tm, t)
    n = w_qkv.shape[1]
    nq = DIL_HEADS * DIL_HEAD_DIM
    colscale = jnp.concatenate([jnp.full((nq,), DIL_HEAD_DIM ** -0.5 * LOG2E, F32), jnp.ones((n - nq,), F32)])[None]
    vec = pl.BlockSpec((1, tn), lambda i, j: (0, j))
    rates = tuple(rate for _, rate in DIL_PAIRS)

    def qkv_epilogue(y, extras, outs):
        (sc_ref,) = extras
        stage = outs[-1]
        y = y * sc_ref[...]
        for k in range(tn // LANES):
            stage[k] = y[:, k * LANES:(k + 1) * LANES]
        for o_ref, rate in zip(outs[:-1], rates):
            if rate == 1:
                o_ref[0] = y.astype(BF16)
            else:
                for c in range(rate):
                    for k in range(tn // LANES):
                        rows = stage[k, pl.ds(c, tm // rate, stride=rate), :]
                        o_ref[c, :, k * LANES:(k + 1) * LANES] = rows.astype(BF16)

    qkvs = _proj(
        "dil_qkv", h, gain, w_qkv.astype(BF16),
        extras=[(colscale, vec)],
        outs=[((rate, t // rate, n), BF16, pl.BlockSpec((rate, tm // rate, tn), lambda i, j: (0, i, j)))
              for rate in rates],
        epilogue=qkv_epilogue, tm=tm, tn=tn, scratch=[pltpu.VMEM((tn // LANES, tm, LANES), F32)])
    npairs = 2
    groups = DIL_HEADS // (2 * npairs)
    os_, ms, ls = [], [], []
    for (window, rate), qkv in zip(DIL_PAIRS, qkvs):
        n_keys = window // rate
        tab = _band_table(rel_bias, rate, n_keys)
        o, m, l = _band(
            f"dil_attn_r{rate}", qkv.reshape(t, n),
            lambda j: j, lambda j: groups + j, lambda j: 2 * groups + j,
            groups, npairs, 1, tab, lambda j: j, seq_len=t // rate)
        os_.append(o)
        ms.append(m)
        ls.append(l)
    expand = jnp.repeat(jnp.eye(LANES, DIL_HEADS, dtype=BF16), DIL_HEAD_DIM, axis=1)
    return _dil_out(os_, ms, ls, rates, expand, w_o.astype(BF16), h)


def _fox(h, gain, w_in, b_f, w_o, tm=1024, tn=512):
    t, dm = h.shape
    tm = min(tm, t)
    hh = FOX_HEADS
    hd = hh * FOX_HEAD_DIM
    heads_per_tile = tn // FOX_HEAD_DIM
    colscale = jnp.concatenate([jnp.full((hd,), FOX_HEAD_DIM ** -0.5 * LOG2E, F32), jnp.ones((hd,), F32)])[None]
    w_all = jnp.pad(w_in, ((0, 0), (0, 3 * hd + LANES - w_in.shape[1]))).astype(BF16)

    def qk_epilogue(y, extras, outs):
        (sc_ref,) = extras
        (o_ref,) = outs
        y = y * sc_ref[...]
        for a in range(heads_per_tile):
            o_ref[a] = y[:, a * FOX_HEAD_DIM:(a + 1) * FOX_HEAD_DIM].astype(BF16)

    (qk,) = _proj(
        "fox_qk", h, gain, w_all,
        extras=[(colscale, pl.BlockSpec((1, tn), lambda i, j: (0, j)))],
        outs=[((2 * hh, t, FOX_HEAD_DIM), BF16,
               pl.BlockSpec((heads_per_tile, tm, FOX_HEAD_DIM), lambda i, j: (j, i, 0)))],
        epilogue=qk_epilogue, tm=tm, tn=tn, col0=0, n=2 * hd)

    def v_epilogue(y, extras, outs):
        (o_ref,) = outs
        for a in range(heads_per_tile):
            o_ref[a, :, :LANES] = y[:, a * FOX_HEAD_DIM:(a + 1) * FOX_HEAD_DIM].astype(BF16)
            o_ref[a, :, LANES:] = jnp.ones((tm, LANES), BF16)

    (v_aug,) = _proj(
        "fox_v", h, gain, w_all,
        extras=[],
        outs=[((hh, t, 2 * LANES), BF16, pl.BlockSpec((heads_per_tile, tm, 2 * LANES), lambda i, j: (j, i, 0)))],
        epilogue=v_epilogue, tm=tm, tn=tn, col0=2 * hd, n=hd)

    def gate_epilogue(y, extras, outs):
        (b_ref,) = extras
        (o_ref,) = outs
        x = y + b_ref[...]
        o_ref[...] = jnp.minimum(x, 0.0) - jnp.log(1.0 + jnp.exp(-jnp.abs(x)))

    b_gate = jnp.pad(b_f.astype(F32), (0, LANES - hh))[None]
    (log_f,) = _proj(
        "fox_gate", h, gain, w_all,
        extras=[(b_gate, pl.BlockSpec((1, LANES), lambda i, j: (0, 0)))],
        outs=[((t, LANES), F32, pl.BlockSpec((tm, LANES), lambda i, j: (i, 0)))],
        epilogue=gate_epilogue, tm=tm, tn=LANES, col0=3 * hd, n=LANES)
    cum, cum_t = _cumsum(log_f)
    o = _flash("fox_attn", qk, qk, v_aug, 0, hh, 0, hh, cum=cum, cumt=cum_t)
    return _out_proj("fox_out", o, w_o.astype(BF16), jnp.zeros((1, dm), F32), h)


def kernel(x, positions, rel_bias, norm_mix, norm_ffn, mla_w_in, mla_g_q, mla_g_kv, mla_w_qb, mla_w_kvb, mla_w_o, swa_w_qkv, swa_b_qkv, swa_sinks, swa_w_o, swa_b_o, dil_w_qkv, dil_w_o, fox_w_in, fox_b_f, fox_w_o, ffn_w_in, ffn_conv_w, ffn_conv_b, ffn_w_out, final_norm):
    bsz, t, dm = x.shape
    assert bsz == 1
    depth = norm_mix.shape[0]
    h = x.reshape(t, dm)
    pos = positions.reshape(t)
    ffn_w_in_b = ffn_w_in.astype(BF16)
    ffn_w_out_b = ffn_w_out.astype(BF16)
    for i in range(depth):
        kind = i % 4
        r = i // 4
        if kind == 0:
            h = _mla(h, norm_mix[i], pos, mla_w_in[r], mla_g_q[r], mla_g_kv[r], mla_w_qb[r], mla_w_kvb[r], mla_w_o[r])
        elif kind == 1:
            h = _swa(h, norm_mix[i], rel_bias, swa_w_qkv[r], swa_b_qkv[r], swa_sinks[r], swa_w_o[r], swa_b_o[r])
        elif kind == 2:
            h = _dilated(h, norm_mix[i], rel_bias, dil_w_qkv[r], dil_w_o[r])
        else:
            h = _fox(h, norm_mix[i], fox_w_in[r], fox_b_f[r], fox_w_o[r])
        h = _ffn(h, norm_ffn[i], i, ffn_w_in_b, ffn_conv_w, ffn_conv_b, ffn_w_out_b, final_norm,
                 final=(i == depth - 1))
    return h.reshape(bsz, t, dm)
```

```python
import functools
import math

import jax
import jax.numpy as jnp
from jax import lax
from jax.experimental import pallas as pl
from jax.experimental.pallas import tpu as pltpu

F32 = jnp.float32
BF16 = jnp.bfloat16
EPS = 1e-6
NEG = -1e30
LOG2E = math.log2(math.e)

LANES = 128
VMEM_LIMIT_BYTES = 56 * 1024 * 1024
BLK = 128

ROPE_THETA = 10000.0
REL_BUCKETS = 32
REL_MAX_DIST = 2048
MLA_HEADS, MLA_Q_RANK, MLA_KV_RANK, MLA_NOPE, MLA_ROPE, MLA_V = 16, 512, 512, 128, 64, 128
SWA_Q_HEADS, SWA_KV_HEADS, SWA_HEAD_DIM, SWA_WINDOW = 32, 4, 64, 128
DIL_HEADS, DIL_HEAD_DIM = 32, 64
DIL_PAIRS = ((128, 1), (512, 4), (2048, 16))
FOX_HEADS, FOX_HEAD_DIM = 16, 128
CONV_WIDTH = 3
CONV_HALO = 16
FFN_SLAB = 512


def _cparams(*sem):
    return pltpu.CompilerParams(dimension_semantics=sem, vmem_limit_bytes=VMEM_LIMIT_BYTES)


def _rms(x, g):
    return x * lax.rsqrt(jnp.mean(x * x, axis=-1, keepdims=True) + EPS) * g


def _dot(a, b):
    return jnp.dot(a, b, preferred_element_type=F32)


def _dot_nt(a, b):
    return lax.dot_general(a, b, (((1,), (1,)), ((), ())), preferred_element_type=F32)


def _rope_tab_kernel(pos_ref, f_ref, sgn_ref, c_ref, s_ref):
    ang = pos_ref[...].astype(F32) * f_ref[...]
    c_ref[...] = jnp.cos(ang)
    s_ref[...] = jnp.sin(ang) * sgn_ref[...]


def _rope_tables(positions):
    t = positions.shape[0]
    half = MLA_ROPE // 2
    inv_freq = ROPE_THETA ** (-jnp.arange(half, dtype=F32) / half)
    zeros = jnp.zeros((LANES - 2 * half,), F32)
    freq = jnp.concatenate([inv_freq, inv_freq, zeros])[None]
    sgn = jnp.concatenate([-jnp.ones((half,), F32), jnp.ones((half,), F32), zeros])[None]
    tm = min(t, 1024)
    row = pl.BlockSpec((tm, LANES), lambda i: (i, 0))
    vec = pl.BlockSpec((1, LANES), lambda i: (0, 0))
    return pl.pallas_call(
        _rope_tab_kernel,
        out_shape=(jax.ShapeDtypeStruct((t, LANES), F32),) * 2,
        grid=(t // tm,),
        in_specs=[pl.BlockSpec((tm, 1), lambda i: (i, 0)), vec, vec],
        out_specs=(row, row),
        compiler_params=_cparams("arbitrary"),
        name="rope_tables",
    )(positions.reshape(t, 1), freq, sgn)


def _rope128(z, c, s):
    half = MLA_ROPE // 2
    lane = lax.broadcasted_iota(jnp.int32, z.shape, 1)
    swapped = jnp.where(lane < half, pltpu.roll(z, LANES - half, 1), pltpu.roll(z, half, 1))
    return z * c + swapped * s


def _proj_kernel(*refs, norm, n_extra, epilogue):
    if norm:
        lhs_ref, g_ref, w_ref = refs[:3]
        rest = refs[3:]
        a_ref = rest[-1]
        rest = rest[:-1]

        @pl.when(pl.program_id(1) == 0)
        def _():
            a_ref[...] = _rms(lhs_ref[...], g_ref[...]).astype(BF16)

        a = a_ref[...]
    else:
        lhs_ref, w_ref = refs[:2]
        rest = refs[2:]
        a = lhs_ref[...]
    y = _dot(a, w_ref[...])
    epilogue(y, rest[:n_extra], rest[n_extra:])


def _proj(name, lhs, gain, w, extras, outs, epilogue, tm, tn, col0=0, n=None, scratch=()):
    m, k = lhs.shape
    n = w.shape[1] if n is None else n
    woff = col0 // tn
    norm = gain is not None
    in_specs = [pl.BlockSpec((tm, k), lambda i, j: (i, 0))]
    args = [lhs]
    if norm:
        in_specs.append(pl.BlockSpec((1, k), lambda i, j: (0, 0)))
        args.append(gain.reshape(1, k))
    in_specs.append(pl.BlockSpec((k, tn), lambda i, j: (0, woff + j)))
    args.append(w)
    for arr, spec in extras:
        in_specs.append(spec)
        args.append(arr)
    res = pl.pallas_call(
        functools.partial(_proj_kernel, norm=norm, n_extra=len(extras), epilogue=epilogue),
        out_shape=tuple(jax.ShapeDtypeStruct(s, d) for s, d, _ in outs),
        grid=(m // tm, n // tn),
        in_specs=in_specs,
        out_specs=tuple(spec for _, _, spec in outs),
        scratch_shapes=list(scratch) + ([pltpu.VMEM((tm, k), BF16)] if norm else []),
        compiler_params=_cparams("arbitrary", "arbitrary"),
        name=name,
    )(*args)
    return res


def _out_proj_kernel(lhs_ref, w_ref, b_ref, res_ref, o_ref):
    o_ref[...] = res_ref[...] + (_dot(lhs_ref[...], w_ref[...]) + b_ref[...])


def _out_proj(name, lhs, w, bias, res, tm=512):
    m, k = lhs.shape
    n = w.shape[1]
    tm = min(tm, m)
    return pl.pallas_call(
        _out_proj_kernel,
        out_shape=jax.ShapeDtypeStruct((m, n), F32),
        grid=(m // tm,),
        in_specs=[
            pl.BlockSpec((tm, k), lambda i: (i, 0)),
            pl.BlockSpec((k, n), lambda i: (0, 0)),
            pl.BlockSpec((1, n), lambda i: (0, 0)),
            pl.BlockSpec((tm, n), lambda i: (i, 0)),
        ],
        out_specs=pl.BlockSpec((tm, n), lambda i: (i, 0)),
        compiler_params=_cparams("arbitrary"),
        name=name,
    )(lhs, w, bias, res)


def _flash_kernel(*refs, bq, fox):
    if fox:
        q_ref, k_ref, v_ref, cum_ref, cumt_ref, o_ref, s0_sc, s1_sc, m_sc, acc_sc = refs
    else:
        q_ref, k_ref, v_ref, o_ref, s0_sc, s1_sc, m_sc, acc_sc = refs
    h = pl.program_id(0)
    i = pl.program_id(1)
    reps = bq // LANES
    m_sc[...] = jnp.full(m_sc.shape, NEG, F32)
    acc_sc[...] = jnp.zeros(acc_sc.shape, F32)
    q = q_ref[0]
    if fox:
        lane = lax.broadcasted_iota(jnp.int32, (bq, LANES), 1)
        cq = jnp.sum(jnp.where(lane == h, cum_ref[...], 0.0), axis=1, keepdims=True)
        cq = jnp.broadcast_to(cq, (bq, LANES))

    def scores(c, dst):
        k0 = pl.multiple_of(c * bq, bq)
        dst[...] = _dot_nt(q, k_ref[0, pl.ds(k0, bq), :])

    def softmax_pv(c, src, masked):
        k0 = pl.multiple_of(c * bq, bq)
        t = src[...]
        if fox:
            t = t - cumt_ref[pl.ds(h, 1), pl.ds(k0, bq)]
        if masked:
            row = lax.broadcasted_iota(jnp.int32, (bq, bq), 0)
            col = lax.broadcasted_iota(jnp.int32, (bq, bq), 1)
            t = jnp.where(col <= row, t, NEG)
        r = jnp.max(t, axis=1, keepdims=True)
        m_prev = m_sc[...]
        m_new = jnp.maximum(m_prev, r + cq) if fox else jnp.maximum(m_prev, r)
        alpha = jnp.exp2(m_prev - m_new)
        shift = m_new - cq if fox else m_new
        p = jnp.exp2(t - jnp.tile(shift, (1, reps)))
        pv = _dot(p.astype(BF16), v_ref[0, pl.ds(k0, bq), :])
        acc_sc[...] = jnp.tile(alpha, (1, 2)) * acc_sc[...] + pv
        m_sc[...] = m_new

    scores(0, s0_sc)

    def pair(c):
        scores(c + 1, s1_sc)
        softmax_pv(c, s0_sc, False)
        scores(c + 2, s0_sc)
        softmax_pv(c + 1, s1_sc, False)

    def body(u, carry):
        pair(4 * u)
        pair(4 * u + 2)
        return carry

    n_quads = lax.shift_right_logical(i, 2)
    lax.fori_loop(0, n_quads, body, 0)

    @pl.when(lax.bitwise_and(i, 2) == 2)
    def _():
        pair(4 * n_quads)

    @pl.when(lax.bitwise_and(i, 1) == 1)
    def _():
        scores(i, s1_sc)
        softmax_pv(i - 1, s0_sc, False)
        softmax_pv(i, s1_sc, True)

    @pl.when(lax.bitwise_and(i, 1) == 0)
    def _():
        softmax_pv(i, s0_sc, True)

    acc = acc_sc[...]
    o_ref[...] = (acc[:, :LANES] / acc[:, LANES:]).astype(BF16)


def _flash(name, q, k, v, q_off, k_off, v_off, heads, cum=None, cumt=None, bq=512):
    t = q.shape[1]
    dk = q.shape[2]
    dv = v.shape[2]
    assert dv == 2 * LANES
    bq = min(bq, t)
    fox = cum is not None
    in_specs = [
        pl.BlockSpec((1, bq, dk), lambda h, i: (q_off + h, i, 0)),
        pl.BlockSpec((1, t, dk), lambda h, i: (k_off + h, 0, 0)),
        pl.BlockSpec((1, t, dv), lambda h, i: (v_off + h, 0, 0)),
    ]
    args = [q, k, v]
    if fox:
        in_specs += [
            pl.BlockSpec((bq, LANES), lambda h, i: (i, 0)),
            pl.BlockSpec(cumt.shape, lambda h, i: (0, 0)),
        ]
        args += [cum, cumt]
    return pl.pallas_call(
        functools.partial(_flash_kernel, bq=bq, fox=fox),
        out_shape=jax.ShapeDtypeStruct((t, heads * LANES), BF16),
        grid=(heads, t // bq),
        in_specs=in_specs,
        out_specs=pl.BlockSpec((bq, LANES), lambda h, i: (i, h)),
        scratch_shapes=[pltpu.VMEM((bq, bq), F32), pltpu.VMEM((bq, bq), F32),
                        pltpu.VMEM((bq, LANES), F32), pltpu.VMEM((bq, dv), F32)],
        compiler_params=_cparams("arbitrary", "arbitrary"),
        name=name,
    )(*args)


def _band_kernel(*refs, nbk, npairs, grp, sinks, tiles_per_seq):
    if sinks:
        q_ref, k_ref, v_ref, kp_ref, vp_ref, tab_ref, sink_ref, o_ref, kf_sc, vf_sc = refs
    else:
        q_ref, k_ref, v_ref, kp_ref, vp_ref, tab_ref, o_ref, m_ref, l_ref, kf_sc, vf_sc = refs
    i = pl.program_id(0)
    j = pl.program_id(1)
    kf_sc[0:BLK, :] = kp_ref[...]
    kf_sc[BLK:, :] = k_ref[...]
    vf_sc[0:BLK, :] = vp_ref[...]
    vf_sc[BLK:, :] = v_ref[...]
    d = LANES // 2
    n_kv = 2 * npairs
    col = lax.broadcasted_iota(jnp.int32, (1, 2 * BLK), 1)
    penrow = jnp.where(jnp.logical_and(col < BLK, i % tiles_per_seq == 0), NEG, 0.0)
    if not sinks:
        stat_lane = lax.broadcasted_iota(jnp.int32, (BLK, LANES), 1)
        steps_per_stat = LANES // n_kv

        @pl.when(j % steps_per_stat == 0)
        def _():
            m_ref[...] = jnp.zeros(m_ref.shape, F32)
            l_ref[...] = jnp.zeros(l_ref.shape, F32)

    lo_k = lax.broadcasted_iota(jnp.int32, (2 * BLK, LANES), 1) < d
    lo_o = lax.broadcasted_iota(jnp.int32, (BLK, LANES), 1) < d
    swap = lambda a: jnp.concatenate([a[:, d:], a[:, :d]], axis=1)

    for b in range(nbk):
        r0 = b * BLK
        if not sinks:
            m_st = m_ref[r0:r0 + BLK, :]
            l_st = l_ref[r0:r0 + BLK, :]
        for kp in range(npairs):
            kk = kf_sc[r0:r0 + 2 * BLK, kp * LANES:(kp + 1) * LANES]
            vv = vf_sc[r0:r0 + 2 * BLK, kp * LANES:(kp + 1) * LANES]
            zero = jnp.zeros_like(kk)
            k_own = [jnp.where(lo_k, kk, zero), jnp.where(lo_k, zero, kk)]
            if grp > 1:
                k_by = [[k_own[0], swap(k_own[0])], [swap(k_own[1]), k_own[1]]]
                v_by = [[vv, swap(vv)], [swap(vv), vv]]
            else:
                k_by = [[k_own[0], None], [None, k_own[1]]]
                v_by = [[vv, None], [None, vv]]
            for c in range(grp):
                qcol = kp * grp + c
                qq = q_ref[r0:r0 + BLK, qcol * LANES:(qcol + 1) * LANES]
                halves = []
                for qh in range(2):
                    hq = 2 * qcol + qh
                    kh = hq // grp - 2 * kp
                    s = _dot_nt(qq, k_by[kh][qh]) + tab_ref[hq]
                    if b == 0:
                        s = s + penrow
                    m = jnp.max(s, axis=1, keepdims=True)
                    p = jnp.exp2(s - m)
                    l = jnp.sum(p, axis=1, keepdims=True)
                    acc = _dot(p.astype(BF16), v_by[kh][qh])
                    if sinks:
                        sink = sink_ref[:, qcol * LANES:(qcol + 1) * LANES]
                        m2 = jnp.maximum(m, sink)
                        a = jnp.exp2(m - m2)
                        den = l * a + jnp.exp2(sink - m2)
                        halves.append(acc * (a / den))
                    else:
                        halves.append(acc / l)
                        sel = stat_lane == (n_kv * j + hq) % LANES
                        m_st = jnp.where(sel, m, m_st)
                        l_st = jnp.where(sel, l, l_st)
                o_ref[r0:r0 + BLK, qcol * LANES:(qcol + 1) * LANES] = jnp.where(lo_o, halves[0], halves[1]).astype(BF16)
        if not sinks:
            m_ref[r0:r0 + BLK, :] = m_st
            l_ref[r0:r0 + BLK, :] = l_st


def _band(name, x, q_blk, k_blk, v_blk, n_steps, npairs, grp, tab, tab_blk, sink_rep=None, tile=512, seq_len=None):
    length = x.shape[0]
    seq_len = length if seq_len is None else seq_len
    tb = min(tile, seq_len)
    nbk = tb // BLK
    kw = LANES * npairs
    qw = kw * grp
    n_kv = 2 * npairs
    n_out = n_steps * qw
    sinks = sink_rep is not None
    prev = lambda i, j: jnp.maximum(i * nbk - 1, 0)
    in_specs = [
        pl.BlockSpec((tb, qw), lambda i, j: (i, q_blk(j))),
        pl.BlockSpec((tb, kw), lambda i, j: (i, k_blk(j))),
        pl.BlockSpec((tb, kw), lambda i, j: (i, v_blk(j))),
        pl.BlockSpec((BLK, kw), lambda i, j: (prev(i, j), k_blk(j))),
        pl.BlockSpec((BLK, kw), lambda i, j: (prev(i, j), v_blk(j))),
        pl.BlockSpec((n_kv * grp, BLK, 2 * BLK), lambda i, j: (tab_blk(j), 0, 0)),
    ]
    args = [x, x, x, x, x, tab]
    out_shape = [jax.ShapeDtypeStruct((length, n_out), BF16)]
    out_specs = [pl.BlockSpec((tb, qw), lambda i, j: (i, j))]
    if sinks:
        in_specs.append(pl.BlockSpec((1, qw), lambda i, j: (0, j)))
        args.append(sink_rep)
    else:
        n_stat = max(LANES, n_kv * n_steps)
        stat = pl.BlockSpec((tb, LANES), lambda i, j: (i, j // (LANES // n_kv)))
        out_shape += [jax.ShapeDtypeStruct((length, n_stat), F32)] * 2
        out_specs += [stat, stat]
    return pl.pallas_call(
        functools.partial(_band_kernel, nbk=nbk, npairs=npairs, grp=grp, sinks=sinks, tiles_per_seq=seq_len // tb),
        out_shape=tuple(out_shape),
        grid=(length // tb, n_steps),
        in_specs=in_specs,
        out_specs=tuple(out_specs),
        scratch_shapes=[pltpu.VMEM((tb + BLK, kw), BF16), pltpu.VMEM((tb + BLK, kw), BF16)],
        compiler_params=_cparams("arbitrary", "arbitrary"),
        name=name,
    )(*args)


def _t5_bucket(n):
    exact = REL_BUCKETS // 2
    nf = jnp.maximum(n, 1).astype(F32)
    large = exact + (jnp.log(nf / exact) / math.log(REL_MAX_DIST / exact) * (REL_BUCKETS - exact)).astype(jnp.int32)
    return jnp.where(n < exact, n, jnp.minimum(large, REL_BUCKETS - 1))


def _band_table(rel_bias, rate, max_dist):
    nh = rel_bias.shape[1]
    w = 2 * BLK
    dist = jnp.arange(w)
    row = rel_bias[_t5_bucket(rate * jnp.minimum(dist, max_dist))].T * LOG2E
    f = jnp.where(dist[None] <= max_dist, row, NEG).astype(F32)
    r = jnp.roll(f[:, ::-1], BLK + 1, axis=1)
    rr = jnp.concatenate([r, r], axis=1)
    flat = jnp.broadcast_to(rr[:, None, :], (nh, BLK + 1, 2 * w)).reshape(nh, (BLK + 1) * 2 * w)
    return flat[:, w:w + BLK * (2 * w - 1)].reshape(nh, BLK, 2 * w - 1)[:, :, :w]


def _dil_out_kernel(o1, o2, o3, m1, m2, m3, l1, l2, l3, e_ref, w_ref, res_ref, out_ref, row_sc, stat_sc, *, tm, rates):
    def in_order(ref, rate, sc):
        if rate == 1:
            return ref[0].astype(F32)
        groups = ref.shape[2] // LANES
        for c in range(rate):
            rows = ref[c].astype(F32)
            for k in range(groups):
                sc[k, pl.ds(c, tm // rate, stride=rate), :] = rows[:, k * LANES:(k + 1) * LANES]
        return jnp.concatenate([sc[k] for k in range(groups)], axis=1)

    ms = [in_order(m, r, stat_sc) for m, r in zip((m1, m2, m3), rates)]
    ls = [in_order(l, r, stat_sc) for l, r in zip((l1, l2, l3), rates)]
    lane = lax.broadcasted_iota(jnp.int32, ms[0].shape, 1)
    mx = jnp.maximum(jnp.maximum(ms[0], ms[1]), ms[2])
    ws = [l * jnp.exp2(m - mx) for m, l in zip(ms, ls)]
    den = ws[0] + ws[1] + ws[2]
    e = e_ref[...]
    o = None
    for w, o_ref, r in zip(ws, (o1, o2, o3), rates):
        wn = jnp.where(lane < DIL_HEADS, w / den, 0.0)
        hi = wn.astype(BF16)
        lo = (wn - hi.astype(F32)).astype(BF16)
        term = in_order(o_ref, r, row_sc) * (_dot(hi, e) + _dot(lo, e))
        o = term if o is None else o + term
    out_ref[...] = res_ref[...] + _dot(o.astype(BF16), w_ref[...])


def _dil_out(os_, ms, ls, rates, expand, w, res, tm=512):
    t, n = res.shape
    tm = min(tm, t)
    row = pl.BlockSpec((tm, n), lambda i: (i, 0))

    def by_residue(a, rate):
        width = a.shape[1]
        return a.reshape(rate, t // rate, width), pl.BlockSpec((rate, tm // rate, width), lambda i: (0, i, 0))

    branch = [by_residue(a, r) for group in (os_, ms, ls) for a, r in zip(group, rates)]
    return pl.pallas_call(
        functools.partial(_dil_out_kernel, tm=tm, rates=tuple(rates)),
        out_shape=jax.ShapeDtypeStruct((t, n), F32),
        grid=(t // tm,),
        in_specs=[spec for _, spec in branch] + [
            pl.BlockSpec((LANES, n), lambda i: (0, 0)),
            pl.BlockSpec((n, n), lambda i: (0, 0)),
            row,
        ],
        out_specs=row,
        scratch_shapes=[pltpu.VMEM((n // LANES, tm, LANES), F32), pltpu.VMEM((1, tm, LANES), F32)],
        compiler_params=_cparams("arbitrary"),
        name="dil_out",
    )(*[a for a, _ in branch], expand, w, res)


def _cumsum_kernel(x_ref, c_ref, ct_ref, carry_sc, *, tb):
    @pl.when(pl.program_id(0) == 0)
    def _():
        carry_sc[...] = jnp.zeros(carry_sc.shape, F32)

    x = x_ref[...]
    row = lax.broadcasted_iota(jnp.int32, (tb, tb), 0)
    colm = lax.broadcasted_iota(jnp.int32, (tb, tb), 1)
    tri = jnp.where(colm <= row, 1.0, 0.0).astype(BF16)
    hi = x.astype(BF16)
    r1 = x - hi.astype(F32)
    mid = r1.astype(BF16)
    lo = (r1 - mid.astype(F32)).astype(BF16)
    c = carry_sc[...] + (_dot(tri, hi) + _dot(tri, mid) + _dot(tri, lo))
    c2 = c * LOG2E
    c_ref[...] = c2
    ct_ref[...] = c2.T
    carry_sc[...] = c[tb - 1:tb, :]


def _cumsum(x, tb=256):
    t = x.shape[0]
    tb = min(tb, t)
    return pl.pallas_call(
        functools.partial(_cumsum_kernel, tb=tb),
        out_shape=(jax.ShapeDtypeStruct((t, LANES), F32), jax.ShapeDtypeStruct((LANES, t), F32)),
        grid=(t // tb,),
        in_specs=[pl.BlockSpec((tb, LANES), lambda i: (i, 0))],
        out_specs=(pl.BlockSpec((tb, LANES), lambda i: (i, 0)), pl.BlockSpec((LANES, tb), lambda i: (0, i))),
        scratch_shapes=[pltpu.VMEM((1, LANES), F32)],
        compiler_params=_cparams("arbitrary"),
        name="fox_cumsum",
    )(x)


def _ffn_kernel(h_ref, hp_ref, g_ref, wg_ref, wv_ref, cwg_ref, cwv_ref, cbg_ref, cbv_ref, wo_ref, gf_ref, o_ref, a_sc,
                *, tm, final):
    i = pl.program_id(0)
    j = pl.program_id(1)

    @pl.when(j == 0)
    def _():
        g = g_ref[...]
        halo = _rms(hp_ref[...], g)
        a_sc[0:CONV_HALO, :] = jnp.where(i == 0, 0.0, halo).astype(BF16)
        a_sc[CONV_HALO:, :] = _rms(h_ref[...], g).astype(BF16)
        o_ref[...] = h_ref[...]

    a = a_sc[...]

    def conv(w_ref, cw_ref, cb_ref, lo, hi):
        u = _dot(a, w_ref[:, lo:hi])
        cw = cw_ref[:, lo:hi]
        c = cb_ref[:, lo:hi] + pltpu.roll(u, 2, 0)[CONV_HALO:] * cw[0:1]
        c = c + pltpu.roll(u, 1, 0)[CONV_HALO:] * cw[1:2]
        return c + u[CONV_HALO:] * cw[2:3]

    tf = wo_ref.shape[0]
    contrib = None
    for lo in range(0, tf, FFN_SLAB):
        hi = lo + FFN_SLAB
        gate = conv(wg_ref, cwg_ref, cbg_ref, lo, hi)
        val = conv(wv_ref, cwv_ref, cbv_ref, lo, hi)
        act = (gate / (1.0 + jnp.exp(-gate))) * val
        part = _dot(act.astype(BF16), wo_ref[lo:hi, :])
        contrib = part if contrib is None else contrib + part
    o_ref[...] += contrib

    if final:
        @pl.when(j == pl.num_programs(1) - 1)
        def _():
            o_ref[...] = _rms(o_ref[...], gf_ref[...])


def _ffn(h, gain, layer, w_in, conv_w, conv_b, w_out, final_gain, final, tm=1024, tf=512):
    t, dm = h.shape
    ff = w_out.shape[1]
    tm = min(tm, t)
    nf = ff // tf
    halo_blocks = tm // CONV_HALO
    once = pl.Buffered(1)
    return pl.pallas_call(
        functools.partial(_ffn_kernel, tm=tm, final=final),
        out_shape=jax.ShapeDtypeStruct((t, dm), F32),
        grid=(t // tm, nf),
        in_specs=[
            pl.BlockSpec((tm, dm), lambda i, j: (i, 0), pipeline_mode=once),
            pl.BlockSpec((CONV_HALO, dm), lambda i, j: (jnp.maximum(i * halo_blocks - 1, 0), 0)),
            pl.BlockSpec((1, dm), lambda i, j: (0, 0)),
            pl.BlockSpec((None, dm, tf), lambda i, j: (layer, 0, j)),
            pl.BlockSpec((None, dm, tf), lambda i, j: (layer, 0, j + nf)),
            pl.BlockSpec((None, CONV_WIDTH, tf), lambda i, j: (layer, 0, j)),
            pl.BlockSpec((None, CONV_WIDTH, tf), lambda i, j: (layer, 0, j + nf)),
            pl.BlockSpec((None, 1, tf), lambda i, j: (layer, 0, j)),
            pl.BlockSpec((None, 1, tf), lambda i, j: (layer, 0, j + nf)),
            pl.BlockSpec((None, tf, dm), lambda i, j: (layer, j, 0)),
            pl.BlockSpec((1, dm), lambda i, j: (0, 0)),
        ],
        out_specs=pl.BlockSpec((tm, dm), lambda i, j: (i, 0)),
        scratch_shapes=[pltpu.VMEM((tm + CONV_HALO, dm), BF16)],
        compiler_params=_cparams("arbitrary", "arbitrary"),
        name="conv_ffn",
    )(h, h, gain.reshape(1, dm), w_in, w_in, conv_w, conv_w, conv_b[:, None, :], conv_b[:, None, :], w_out,
      final_gain.reshape(1, dm))


def _mla(h, gain, positions, w_in, g_q, g_kv, w_qb, w_kvb, w_o, tm=1024):
    t, dm = h.shape
    tm = min(tm, t)
    hh = MLA_HEADS
    qk = MLA_NOPE + MLA_ROPE
    lat_w = MLA_Q_RANK + MLA_KV_RANK + LANES
    cos_t, sin_t = _rope_tables(positions)
    w_in_p = jnp.pad(w_in, ((0, 0), (0, lat_w - w_in.shape[1]))).astype(BF16)
    w_qb_p = jnp.pad(w_qb.reshape(MLA_Q_RANK, hh, qk), ((0, 0), (0, 0), (0, 2 * LANES - qk)))
    w_qb_p = w_qb_p.reshape(MLA_Q_RANK, hh * 2 * LANES).astype(BF16)

    row128 = pl.BlockSpec((tm, LANES), lambda i, j: (i, 0))

    def lat_epilogue(y, extras, outs):
        gq_ref, gkv_ref, c_ref, s_ref = extras
        cq_ref, ckv_ref, kr_ref = outs
        cq_ref[...] = _rms(y[:, :MLA_Q_RANK], gq_ref[...]).astype(BF16)
        ckv_ref[...] = _rms(y[:, MLA_Q_RANK:MLA_Q_RANK + MLA_KV_RANK], gkv_ref[...]).astype(BF16)
        kr_ref[...] = _rope128(y[:, MLA_Q_RANK + MLA_KV_RANK:], c_ref[...], s_ref[...]).astype(BF16)

    rank_row = pl.BlockSpec((tm, MLA_Q_RANK), lambda i, j: (i, 0))
    c_q, c_kv, k_rope = _proj(
        "mla_latents", h, gain, w_in_p,
        extras=[(g_q.reshape(1, -1), pl.BlockSpec((1, MLA_Q_RANK), lambda i, j: (0, 0))),
                (g_kv.reshape(1, -1), pl.BlockSpec((1, MLA_KV_RANK), lambda i, j: (0, 0))),
                (cos_t, row128), (sin_t, row128)],
        outs=[((t, MLA_Q_RANK), BF16, rank_row), ((t, MLA_KV_RANK), BF16, rank_row), ((t, LANES), BF16, row128)],
        epilogue=lat_epilogue, tm=tm, tn=lat_w)

    heads_per_tile = 4
    tn = heads_per_tile * 2 * LANES
    scale = qk ** -0.5 * LOG2E

    def q_epilogue(y, extras, outs):
        c_ref, s_ref = extras
        (q_ref,) = outs
        y = y * scale
        for a in range(heads_per_tile):
            base = a * 2 * LANES
            q_ref[a, :, :LANES] = y[:, base:base + LANES].astype(BF16)
            q_ref[a, :, LANES:] = _rope128(y[:, base + LANES:base + 2 * LANES], c_ref[...], s_ref[...]).astype(BF16)

    head_blk = pl.BlockSpec((heads_per_tile, tm, 2 * LANES), lambda i, j: (j, i, 0))
    (q_cat,) = _proj(
        "mla_q", c_q, None, w_qb_p,
        extras=[(cos_t, row128), (sin_t, row128)],
        outs=[((hh, t, 2 * LANES), BF16, head_blk)],
        epilogue=q_epilogue, tm=tm, tn=tn)

    def kv_epilogue(y, extras, outs):
        (kr_ref,) = extras
        k_ref, v_ref = outs
        for a in range(heads_per_tile):
            base = a * 2 * LANES
            k_ref[a, :, :LANES] = y[:, base:base + LANES].astype(BF16)
            k_ref[a, :, LANES:] = kr_ref[...]
            v_ref[a, :, :LANES] = y[:, base + LANES:base + 2 * LANES].astype(BF16)
            v_ref[a, :, LANES:] = jnp.ones((tm, LANES), BF16)

    k_cat, v = _proj(
        "mla_kv", c_kv, None, w_kvb.astype(BF16),
        extras=[(k_rope, row128)],
        outs=[((hh, t, 2 * LANES), BF16, head_blk), ((hh, t, 2 * LANES), BF16, head_blk)],
        epilogue=kv_epilogue, tm=tm, tn=tn)

    o = _flash("mla_attn", q_cat, k_cat, v, 0, 0, 0, hh)
    return _out_proj("mla_out", o, w_o.astype(BF16), jnp.zeros((1, dm), F32), h)


def _cols_epilogue(y, extras, outs):
    b_ref, sc_ref = extras
    (o_ref,) = outs
    o_ref[...] = ((y + b_ref[...]) * sc_ref[...]).astype(BF16)


def _swa(h, gain, rel_bias, w_qkv, b_qkv, sinks, w_o, b_o, tm=1024, tn=512):
    t, dm = h.shape
    tm = min(tm, t)
    n = w_qkv.shape[1]
    nq = SWA_Q_HEADS * SWA_HEAD_DIM
    grp = SWA_Q_HEADS // SWA_KV_HEADS
    colscale = jnp.concatenate([jnp.full((nq,), SWA_HEAD_DIM ** -0.5 * LOG2E, F32), jnp.ones((n - nq,), F32)])[None]
    vec = pl.BlockSpec((1, tn), lambda i, j: (0, j))
    (qkv,) = _proj(
        "swa_qkv", h, gain, w_qkv.astype(BF16),
        extras=[(b_qkv.reshape(1, n), vec), (colscale, vec)],
        outs=[((t, n), BF16, pl.BlockSpec((tm, tn), lambda i, j: (i, j)))],
        epilogue=_cols_epilogue, tm=tm, tn=tn)
    tab = _band_table(rel_bias, 1, SWA_WINDOW - 1)
    sink_rep = jnp.repeat(sinks.astype(F32) * LOG2E, SWA_HEAD_DIM)[None]
    k0 = nq // LANES
    v0 = k0 + SWA_KV_HEADS * SWA_HEAD_DIM // LANES
    (o,) = _band("swa_attn", qkv, lambda j: j, lambda j: k0 + j, lambda j: v0 + j, SWA_KV_HEADS // 2, 1, grp,
                 tab, lambda j: j, sink_rep=sink_rep, tile=BLK)
    return _out_proj("swa_out", o, w_o.astype(BF16), b_o.reshape(1, dm), h)


def _dilated(h, gain, rel_bias, w_qkv, w_o, tm=1024, tn=512):
    t, dm = h.shape
    tm = min(tm, t)
    n = w_qkv.shape[1]
    nq = DIL_HEADS * DIL_HEAD_DIM
    colscale = jnp.concatenate([jnp.full((nq,), DIL_HEAD_DIM ** -0.5 * LOG2E, F32), jnp.ones((n - nq,), F32)])[None]
    vec = pl.BlockSpec((1, tn), lambda i, j: (0, j))
    rates = tuple(rate for _, rate in DIL_PAIRS)

    def qkv_epilogue(y, extras, outs):
        (sc_ref,) = extras
        stage = outs[-1]
        y = y * sc_ref[...]
        for k in range(tn // LANES):
            stage[k] = y[:, k * LANES:(k + 1) * LANES]
        for o_ref, rate in zip(outs[:-1], rates):
            if rate == 1:
                o_ref[0] = y.astype(BF16)
            else:
                for c in range(rate):
                    for k in range(tn // LANES):
                        rows = stage[k, pl.ds(c, tm // rate, stride=rate), :]
                        o_ref[c, :, k * LANES:(k + 1) * LANES] = rows.astype(BF16)

    qkvs = _proj(
        "dil_qkv", h, gain, w_qkv.astype(BF16),
        extras=[(colscale, vec)],
        outs=[((rate, t // rate, n), BF16, pl.BlockSpec((rate, tm // rate, tn), lambda i, j: (0, i, j)))
              for rate in rates],
        epilogue=qkv_epilogue, tm=tm, tn=tn, scratch=[pltpu.VMEM((tn // LANES, tm, LANES), F32)])
    npairs = 2
    groups = DIL_HEADS // (2 * npairs)
    os_, ms, ls = [], [], []
    for (window, rate), qkv in zip(DIL_PAIRS, qkvs):
        n_keys = window // rate
        tab = _band_table(rel_bias, rate, n_keys)
        o, m, l = _band(
            f"dil_attn_r{rate}", qkv.reshape(t, n),
            lambda j: j, lambda j: groups + j, lambda j: 2 * groups + j,
            groups, npairs, 1, tab, lambda j: j, seq_len=t // rate)
        os_.append(o)
        ms.append(m)
        ls.append(l)
    expand = jnp.repeat(jnp.eye(LANES, DIL_HEADS, dtype=BF16), DIL_HEAD_DIM, axis=1)
    return _dil_out(os_, ms, ls, rates, expand, w_o.astype(BF16), h)


def _fox(h, gain, w_in, b_f, w_o, tm=1024, tn=512):
    t, dm = h.shape
    tm = min(tm, t)
    hh = FOX_HEADS
    hd = hh * FOX_HEAD_DIM
    heads_per_tile = tn // FOX_HEAD_DIM
    colscale = jnp.concatenate([jnp.full((hd,), FOX_HEAD_DIM ** -0.5 * LOG2E, F32), jnp.ones((hd,), F32)])[None]
    w_all = jnp.pad(w_in, ((0, 0), (0, 3 * hd + LANES - w_in.shape[1]))).astype(BF16)

    def qk_epilogue(y, extras, outs):
        (sc_ref,) = extras
        (o_ref,) = outs
        y = y * sc_ref[...]
        for a in range(heads_per_tile):
            o_ref[a] = y[:, a * FOX_HEAD_DIM:(a + 1) * FOX_HEAD_DIM].astype(BF16)

    (qk,) = _proj(
        "fox_qk", h, gain, w_all,
        extras=[(colscale, pl.BlockSpec((1, tn), lambda i, j: (0, j)))],
        outs=[((2 * hh, t, FOX_HEAD_DIM), BF16,
               pl.BlockSpec((heads_per_tile, tm, FOX_HEAD_DIM), lambda i, j: (j, i, 0)))],
        epilogue=qk_epilogue, tm=tm, tn=tn, col0=0, n=2 * hd)

    def v_epilogue(y, extras, outs):
        (o_ref,) = outs
        for a in range(heads_per_tile):
            o_ref[a, :, :LANES] = y[:, a * FOX_HEAD_DIM:(a + 1) * FOX_HEAD_DIM].astype(BF16)
            o_ref[a, :, LANES:] = jnp.ones((tm, LANES), BF16)

    (v_aug,) = _proj(
        "fox_v", h, gain, w_all,
        extras=[],
        outs=[((hh, t, 2 * LANES), BF16, pl.BlockSpec((heads_per_tile, tm, 2 * LANES), lambda i, j: (j, i, 0)))],
        epilogue=v_epilogue, tm=tm, tn=tn, col0=2 * hd, n=hd)

    def gate_epilogue(y, extras, outs):
        (b_ref,) = extras
        (o_ref,) = outs
        x = y + b_ref[...]
        o_ref[...] = jnp.minimum(x, 0.0) - jnp.log(1.0 + jnp.exp(-jnp.abs(x)))

    b_gate = jnp.pad(b_f.astype(F32), (0, LANES - hh))[None]
    (log_f,) = _proj(
        "fox_gate", h, gain, w_all,
        extras=[(b_gate, pl.BlockSpec((1, LANES), lambda i, j: (0, 0)))],
        outs=[((t, LANES), F32, pl.BlockSpec((tm, LANES), lambda i, j: (i, 0)))],
        epilogue=gate_epilogue, tm=tm, tn=LANES, col0=3 * hd, n=LANES)
    cum, cum_t = _cumsum(log_f)
    o = _flash("fox_attn", qk, qk, v_aug, 0, hh, 0, hh, cum=cum, cumt=cum_t)
    return _out_proj("fox_out", o, w_o.astype(BF16), jnp.zeros((1, dm), F32), h)


def kernel(x, positions, rel_bias, norm_mix, norm_ffn, mla_w_in, mla_g_q, mla_g_kv, mla_w_qb, mla_w_kvb, mla_w_o, swa_w_qkv, swa_b_qkv, swa_sinks, swa_w_o, swa_b_o, dil_w_qkv, dil_w_o, fox_w_in, fox_b_f, fox_w_o, ffn_w_in, ffn_conv_w, ffn_conv_b, ffn_w_out, final_norm):
    bsz, t, dm = x.shape
    assert bsz == 1
    depth = norm_mix.shape[0]
    h = x.reshape(t, dm)
    pos = positions.reshape(t)
    ffn_w_in_b = ffn_w_in.astype(BF16)
    ffn_w_out_b = ffn_w_out.astype(BF16)
    for i in range(depth):
        kind = i % 4
        r = i // 4
        if kind == 0:
            h = _mla(h, norm_mix[i], pos, mla_w_in[r], mla_g_q[r], mla_g_kv[r], mla_w_qb[r], mla_w_kvb[r], mla_w_o[r])
        elif kind == 1:
            h = _swa(h, norm_mix[i], rel_bias, swa_w_qkv[r], swa_b_qkv[r], swa_sinks[r], swa_w_o[r], swa_b_o[r])
        elif kind == 2:
            h = _dilated(h, norm_mix[i], rel_bias, dil_w_qkv[r], dil_w_o[r])
        else:
            h = _fox(h, norm_mix[i], fox_w_in[r], fox_b_f[r], fox_w_o[r])
        h = _ffn(h, norm_ffn[i], i, ffn_w_in_b, ffn_conv_w, ffn_conv_b, ffn_w_out_b, final_norm,
                 final=(i == depth - 1))
    return h.reshape(bsz, t, dm)
```

```python
import functools
import math

import jax
import jax.numpy as jnp
from jax import lax
from jax.experimental import pallas as pl
from jax.experimental.pallas import tpu as pltpu

F32 = jnp.float32
BF16 = jnp.bfloat16
EPS = 1e-6
NEG = -1e30
LOG2E = math.log2(math.e)

LANES = 128
VMEM_LIMIT_BYTES = 56 * 1024 * 1024
BLK = 128

ROPE_THETA = 10000.0
REL_BUCKETS = 32
REL_MAX_DIST = 2048
MLA_HEADS, MLA_Q_RANK, MLA_KV_RANK, MLA_NOPE, MLA_ROPE, MLA_V = 16, 512, 512, 128, 64, 128
SWA_Q_HEADS, SWA_KV_HEADS, SWA_HEAD_DIM, SWA_WINDOW = 32, 4, 64, 128
DIL_HEADS, DIL_HEAD_DIM = 32, 64
DIL_PAIRS = ((128, 1), (512, 4), (2048, 16))
FOX_HEADS, FOX_HEAD_DIM = 16, 128
CONV_WIDTH = 3
CONV_HALO = 16
FFN_SLAB = 512
FLASH_HEADS = 1


def _cparams(*sem):
    return pltpu.CompilerParams(dimension_semantics=sem, vmem_limit_bytes=VMEM_LIMIT_BYTES)


def _rms(x, g):
    return x * lax.rsqrt(jnp.mean(x * x, axis=-1, keepdims=True) + EPS) * g


def _dot(a, b):
    return jnp.dot(a, b, preferred_element_type=F32)


def _dot_nt(a, b):
    return lax.dot_general(a, b, (((1,), (1,)), ((), ())), preferred_element_type=F32)


def _rope_tab_kernel(pos_ref, f_ref, sgn_ref, c_ref, s_ref):
    ang = pos_ref[...].astype(F32) * f_ref[...]
    c_ref[...] = jnp.cos(ang)
    s_ref[...] = jnp.sin(ang) * sgn_ref[...]


def _rope_tables(positions):
    t = positions.shape[0]
    half = MLA_ROPE // 2
    inv_freq = ROPE_THETA ** (-jnp.arange(half, dtype=F32) / half)
    zeros = jnp.zeros((LANES - 2 * half,), F32)
    freq = jnp.concatenate([inv_freq, inv_freq, zeros])[None]
    sgn = jnp.concatenate([-jnp.ones((half,), F32), jnp.ones((half,), F32), zeros])[None]
    tm = min(t, 1024)
    row = pl.BlockSpec((tm, LANES), lambda i: (i, 0))
    vec = pl.BlockSpec((1, LANES), lambda i: (0, 0))
    return pl.pallas_call(
        _rope_tab_kernel,
        out_shape=(jax.ShapeDtypeStruct((t, LANES), F32),) * 2,
        grid=(t // tm,),
        in_specs=[pl.BlockSpec((tm, 1), lambda i: (i, 0)), vec, vec],
        out_specs=(row, row),
        compiler_params=_cparams("arbitrary"),
        name="rope_tables",
    )(positions.reshape(t, 1), freq, sgn)


def _rope128(z, c, s):
    half = MLA_ROPE // 2
    lane = lax.broadcasted_iota(jnp.int32, z.shape, 1)
    swapped = jnp.where(lane < half, pltpu.roll(z, LANES - half, 1), pltpu.roll(z, half, 1))
    return z * c + swapped * s


def _proj_kernel(*refs, norm, n_extra, epilogue):
    if norm:
        lhs_ref, g_ref, w_ref = refs[:3]
        rest = refs[3:]
        a_ref = rest[-1]
        rest = rest[:-1]

        @pl.when(pl.program_id(1) == 0)
        def _():
            a_ref[...] = _rms(lhs_ref[...], g_ref[...]).astype(BF16)

        a = a_ref[...]
    else:
        lhs_ref, w_ref = refs[:2]
        rest = refs[2:]
        a = lhs_ref[...]
    y = _dot(a, w_ref[...])
    epilogue(y, rest[:n_extra], rest[n_extra:])


def _proj(name, lhs, gain, w, extras, outs, epilogue, tm, tn, col0=0, n=None, scratch=()):
    m, k = lhs.shape
    n = w.shape[1] if n is None else n
    woff = col0 // tn
    norm = gain is not None
    in_specs = [pl.BlockSpec((tm, k), lambda i, j: (i, 0))]
    args = [lhs]
    if norm:
        in_specs.append(pl.BlockSpec((1, k), lambda i, j: (0, 0)))
        args.append(gain.reshape(1, k))
    in_specs.append(pl.BlockSpec((k, tn), lambda i, j: (0, woff + j)))
    args.append(w)
    for arr, spec in extras:
        in_specs.append(spec)
        args.append(arr)
    res = pl.pallas_call(
        functools.partial(_proj_kernel, norm=norm, n_extra=len(extras), epilogue=epilogue),
        out_shape=tuple(jax.ShapeDtypeStruct(s, d) for s, d, _ in outs),
        grid=(m // tm, n // tn),
        in_specs=in_specs,
        out_specs=tuple(spec for _, _, spec in outs),
        scratch_shapes=list(scratch) + ([pltpu.VMEM((tm, k), BF16)] if norm else []),
        compiler_params=_cparams("arbitrary", "arbitrary"),
        name=name,
    )(*args)
    return res


def _out_proj_kernel(lhs_ref, w_ref, b_ref, res_ref, o_ref):
    o_ref[...] = res_ref[...] + (_dot(lhs_ref[...], w_ref[...]) + b_ref[...])


def _out_proj(name, lhs, w, bias, res, tm=512):
    m, k = lhs.shape
    n = w.shape[1]
    tm = min(tm, m)
    return pl.pallas_call(
        _out_proj_kernel,
        out_shape=jax.ShapeDtypeStruct((m, n), F32),
        grid=(m // tm,),
        in_specs=[
            pl.BlockSpec((tm, k), lambda i: (i, 0)),
            pl.BlockSpec((k, n), lambda i: (0, 0)),
            pl.BlockSpec((1, n), lambda i: (0, 0)),
            pl.BlockSpec((tm, n), lambda i: (i, 0)),
        ],
        out_specs=pl.BlockSpec((tm, n), lambda i: (i, 0)),
        compiler_params=_cparams("arbitrary"),
        name=name,
    )(lhs, w, bias, res)


def _flash_kernel(*refs, bq, fox):
    if fox:
        q_ref, k_ref, v_ref, cum_ref, cumt_ref, o_ref, s_sc, m_sc, acc_sc = refs
    else:
        q_ref, k_ref, v_ref, o_ref, s_sc, m_sc, acc_sc = refs
    nh = q_ref.shape[0]
    g = pl.program_id(0)
    i = pl.program_id(1)
    reps = bq // LANES
    m_sc[...] = jnp.full(m_sc.shape, NEG, F32)
    acc_sc[...] = jnp.zeros(acc_sc.shape, F32)
    q = [q_ref[a] for a in range(nh)]
    if fox:
        lane = lax.broadcasted_iota(jnp.int32, (bq, LANES), 1)
        cq = [jnp.broadcast_to(jnp.sum(jnp.where(lane == nh * g + a, cum_ref[...], 0.0), axis=1, keepdims=True),
                               (bq, LANES)) for a in range(nh)]

    def scores(c, slot):
        k0 = pl.multiple_of(c * bq, bq)
        for a in range(nh):
            s_sc[2 * a + slot] = _dot_nt(q[a], k_ref[a, pl.ds(k0, bq), :])

    def softmax_pv(c, slot, masked):
        k0 = pl.multiple_of(c * bq, bq)
        for a in range(nh):
            t = s_sc[2 * a + slot]
            if fox:
                t = t - cumt_ref[pl.ds(nh * g + a, 1), pl.ds(k0, bq)]
            if masked:
                row = lax.broadcasted_iota(jnp.int32, (bq, bq), 0)
                col = lax.broadcasted_iota(jnp.int32, (bq, bq), 1)
                t = jnp.where(col <= row, t, NEG)
            r = jnp.max(t, axis=1, keepdims=True)
            m_prev = m_sc[a]
            m_new = jnp.maximum(m_prev, r + cq[a]) if fox else jnp.maximum(m_prev, r)
            alpha = jnp.exp2(m_prev - m_new)
            shift = m_new - cq[a] if fox else m_new
            p = jnp.exp2(t - jnp.tile(shift, (1, reps)))
            pv = _dot(p.astype(BF16), v_ref[a, pl.ds(k0, bq), :])
            acc_sc[a] = jnp.tile(alpha, (1, 2)) * acc_sc[a] + pv
            m_sc[a] = m_new

    scores(0, 0)

    def pair(c):
        scores(c + 1, 1)
        softmax_pv(c, 0, False)
        scores(c + 2, 0)
        softmax_pv(c + 1, 1, False)

    def body(u, carry):
        for a in range(4):
            pair(8 * u + 2 * a)
        return carry

    n_octs = lax.shift_right_logical(i, 3)
    lax.fori_loop(0, n_octs, body, 0)
    done = 8 * n_octs

    @pl.when(lax.bitwise_and(i, 4) == 4)
    def _():
        pair(done)
        pair(done + 2)

    done = done + lax.bitwise_and(i, 4)

    @pl.when(lax.bitwise_and(i, 2) == 2)
    def _():
        pair(done)

    @pl.when(lax.bitwise_and(i, 1) == 1)
    def _():
        scores(i, 1)
        softmax_pv(i - 1, 0, False)
        softmax_pv(i, 1, True)

    @pl.when(lax.bitwise_and(i, 1) == 0)
    def _():
        softmax_pv(i, 0, True)

    for a in range(nh):
        acc = acc_sc[a]
        o_ref[:, a * LANES:(a + 1) * LANES] = (acc[:, :LANES] / acc[:, LANES:]).astype(BF16)


def _flash(name, q, k, v, q_off, k_off, v_off, heads, cum=None, cumt=None, bq=512, nh=FLASH_HEADS):
    t = q.shape[1]
    dk = q.shape[2]
    dv = v.shape[2]
    assert dv == 2 * LANES
    assert heads % nh == 0 and q_off % nh == 0 and k_off % nh == 0 and v_off % nh == 0
    bq = min(bq, t)
    fox = cum is not None
    in_specs = [
        pl.BlockSpec((nh, bq, dk), lambda h, i: (q_off // nh + h, i, 0)),
        pl.BlockSpec((nh, t, dk), lambda h, i: (k_off // nh + h, 0, 0)),
        pl.BlockSpec((nh, t, dv), lambda h, i: (v_off // nh + h, 0, 0)),
    ]
    args = [q, k, v]
    if fox:
        in_specs += [
            pl.BlockSpec((bq, LANES), lambda h, i: (i, 0)),
            pl.BlockSpec(cumt.shape, lambda h, i: (0, 0)),
        ]
        args += [cum, cumt]
    return pl.pallas_call(
        functools.partial(_flash_kernel, bq=bq, fox=fox),
        out_shape=jax.ShapeDtypeStruct((t, heads * LANES), BF16),
        grid=(heads // nh, t // bq),
        in_specs=in_specs,
        out_specs=pl.BlockSpec((bq, nh * LANES), lambda h, i: (i, h)),
        scratch_shapes=[pltpu.VMEM((2 * nh, bq, bq), F32), pltpu.VMEM((nh, bq, LANES), F32),
                        pltpu.VMEM((nh, bq, dv), F32)],
        compiler_params=_cparams("arbitrary", "arbitrary"),
        name=name,
    )(*args)


def _band_kernel(*refs, nbk, npairs, grp, sinks, tiles_per_seq):
    if sinks:
        q_ref, k_ref, v_ref, kp_ref, vp_ref, tab_ref, sink_ref, o_ref, kf_sc, vf_sc = refs
    else:
        q_ref, k_ref, v_ref, kp_ref, vp_ref, tab_ref, o_ref, m_ref, l_ref, kf_sc, vf_sc = refs
    i = pl.program_id(0)
    j = pl.program_id(1)
    kf_sc[0:BLK, :] = kp_ref[...]
    kf_sc[BLK:, :] = k_ref[...]
    vf_sc[0:BLK, :] = vp_ref[...]
    vf_sc[BLK:, :] = v_ref[...]
    d = LANES // 2
    n_kv = 2 * npairs
    col = lax.broadcasted_iota(jnp.int32, (1, 2 * BLK), 1)
    penrow = jnp.where(jnp.logical_and(col < BLK, i % tiles_per_seq == 0), NEG, 0.0)
    if not sinks:
        stat_lane = lax.broadcasted_iota(jnp.int32, (BLK, LANES), 1)
        steps_per_stat = LANES // n_kv

        @pl.when(j % steps_per_stat == 0)
        def _():
            m_ref[...] = jnp.zeros(m_ref.shape, F32)
            l_ref[...] = jnp.zeros(l_ref.shape, F32)

    lo_k = lax.broadcasted_iota(jnp.int32, (2 * BLK, LANES), 1) < d
    lo_o = lax.broadcasted_iota(jnp.int32, (BLK, LANES), 1) < d
    swap = lambda a: jnp.concatenate([a[:, d:], a[:, :d]], axis=1)

    for b in range(nbk):
        r0 = b * BLK
        if not sinks:
            m_st = m_ref[r0:r0 + BLK, :]
            l_st = l_ref[r0:r0 + BLK, :]
        for kp in range(npairs):
            kk = kf_sc[r0:r0 + 2 * BLK, kp * LANES:(kp + 1) * LANES]
            vv = vf_sc[r0:r0 + 2 * BLK, kp * LANES:(kp + 1) * LANES]
            zero = jnp.zeros_like(kk)
            k_own = [jnp.where(lo_k, kk, zero), jnp.where(lo_k, zero, kk)]
            if grp > 1:
                k_by = [[k_own[0], swap(k_own[0])], [swap(k_own[1]), k_own[1]]]
                v_by = [[vv, swap(vv)], [swap(vv), vv]]
            else:
                k_by = [[k_own[0], None], [None, k_own[1]]]
                v_by = [[vv, None], [None, vv]]
            for c in range(grp):
                qcol = kp * grp + c
                qq = q_ref[r0:r0 + BLK, qcol * LANES:(qcol + 1) * LANES]
                halves = []
                for qh in range(2):
                    hq = 2 * qcol + qh
                    kh = hq // grp - 2 * kp
                    s = _dot_nt(qq, k_by[kh][qh]) + tab_ref[hq]
                    if b == 0:
                        s = s + penrow
                    m = jnp.max(s, axis=1, keepdims=True)
                    p = jnp.exp2(s - m)
                    l = jnp.sum(p, axis=1, keepdims=True)
                    acc = _dot(p.astype(BF16), v_by[kh][qh])
                    if sinks:
                        sink = sink_ref[:, qcol * LANES:(qcol + 1) * LANES]
                        m2 = jnp.maximum(m, sink)
                        a = jnp.exp2(m - m2)
                        den = l * a + jnp.exp2(sink - m2)
                        halves.append(acc * (a / den))
                    else:
                        halves.append(acc / l)
                        sel = stat_lane == (n_kv * j + hq) % LANES
                        m_st = jnp.where(sel, m, m_st)
                        l_st = jnp.where(sel, l, l_st)
                o_ref[r0:r0 + BLK, qcol * LANES:(qcol + 1) * LANES] = jnp.where(lo_o, halves[0], halves[1]).astype(BF16)
        if not sinks:
            m_ref[r0:r0 + BLK, :] = m_st
            l_ref[r0:r0 + BLK, :] = l_st


def _band(name, x, q_blk, k_blk, v_blk, n_steps, npairs, grp, tab, tab_blk, sink_rep=None, tile=512, seq_len=None):
    length = x.shape[0]
    seq_len = length if seq_len is None else seq_len
    tb = min(tile, seq_len)
    nbk = tb // BLK
    kw = LANES * npairs
    qw = kw * grp
    n_kv = 2 * npairs
    n_out = n_steps * qw
    sinks = sink_rep is not None
    prev = lambda i, j: jnp.maximum(i * nbk - 1, 0)
    in_specs = [
        pl.BlockSpec((tb, qw), lambda i, j: (i, q_blk(j))),
        pl.BlockSpec((tb, kw), lambda i, j: (i, k_blk(j))),
        pl.BlockSpec((tb, kw), lambda i, j: (i, v_blk(j))),
        pl.BlockSpec((BLK, kw), lambda i, j: (prev(i, j), k_blk(j))),
        pl.BlockSpec((BLK, kw), lambda i, j: (prev(i, j), v_blk(j))),
        pl.BlockSpec((n_kv * grp, BLK, 2 * BLK), lambda i, j: (tab_blk(j), 0, 0)),
    ]
    args = [x, x, x, x, x, tab]
    out_shape = [jax.ShapeDtypeStruct((length, n_out), BF16)]
    out_specs = [pl.BlockSpec((tb, qw), lambda i, j: (i, j))]
    if sinks:
        in_specs.append(pl.BlockSpec((1, qw), lambda i, j: (0, j)))
        args.append(sink_rep)
    else:
        n_stat = max(LANES, n_kv * n_steps)
        stat = pl.BlockSpec((tb, LANES), lambda i, j: (i, j // (LANES // n_kv)))
        out_shape += [jax.ShapeDtypeStruct((length, n_stat), F32)] * 2
        out_specs += [stat, stat]
    return pl.pallas_call(
        functools.partial(_band_kernel, nbk=nbk, npairs=npairs, grp=grp, sinks=sinks, tiles_per_seq=seq_len // tb),
        out_shape=tuple(out_shape),
        grid=(length // tb, n_steps),
        in_specs=in_specs,
        out_specs=tuple(out_specs),
        scratch_shapes=[pltpu.VMEM((tb + BLK, kw), BF16), pltpu.VMEM((tb + BLK, kw), BF16)],
        compiler_params=_cparams("arbitrary", "arbitrary"),
        name=name,
    )(*args)


def _t5_bucket(n):
    exact = REL_BUCKETS // 2
    nf = jnp.maximum(n, 1).astype(F32)
    large = exact + (jnp.log(nf / exact) / math.log(REL_MAX_DIST / exact) * (REL_BUCKETS - exact)).astype(jnp.int32)
    return jnp.where(n < exact, n, jnp.minimum(large, REL_BUCKETS - 1))


def _band_table(rel_bias, rate, max_dist):
    nh = rel_bias.shape[1]
    w = 2 * BLK
    dist = jnp.arange(w)
    row = rel_bias[_t5_bucket(rate * jnp.minimum(dist, max_dist))].T * LOG2E
    f = jnp.where(dist[None] <= max_dist, row, NEG).astype(F32)
    r = jnp.roll(f[:, ::-1], BLK + 1, axis=1)
    rr = jnp.concatenate([r, r], axis=1)
    flat = jnp.broadcast_to(rr[:, None, :], (nh, BLK + 1, 2 * w)).reshape(nh, (BLK + 1) * 2 * w)
    return flat[:, w:w + BLK * (2 * w - 1)].reshape(nh, BLK, 2 * w - 1)[:, :, :w]


def _dil_out_kernel(o1, o2, o3, m1, m2, m3, l1, l2, l3, e_ref, w_ref, res_ref, out_ref, row_sc, stat_sc, *, tm, rates):
    def in_order(ref, rate, sc):
        if rate == 1:
            return ref[0].astype(F32)
        groups = ref.shape[2] // LANES
        for c in range(rate):
            rows = ref[c].astype(F32)
            for k in range(groups):
                sc[k, pl.ds(c, tm // rate, stride=rate), :] = rows[:, k * LANES:(k + 1) * LANES]
        return jnp.concatenate([sc[k] for k in range(groups)], axis=1)

    ms = [in_order(m, r, stat_sc) for m, r in zip((m1, m2, m3), rates)]
    ls = [in_order(l, r, stat_sc) for l, r in zip((l1, l2, l3), rates)]
    lane = lax.broadcasted_iota(jnp.int32, ms[0].shape, 1)
    mx = jnp.maximum(jnp.maximum(ms[0], ms[1]), ms[2])
    ws = [l * jnp.exp2(m - mx) for m, l in zip(ms, ls)]
    den = ws[0] + ws[1] + ws[2]
    e = e_ref[...]
    o = None
    for w, o_ref, r in zip(ws, (o1, o2, o3), rates):
        wn = jnp.where(lane < DIL_HEADS, w / den, 0.0)
        hi = wn.astype(BF16)
        lo = (wn - hi.astype(F32)).astype(BF16)
        term = in_order(o_ref, r, row_sc) * (_dot(hi, e) + _dot(lo, e))
        o = term if o is None else o + term
    out_ref[...] = res_ref[...] + _dot(o.astype(BF16), w_ref[...])


def _dil_out(os_, ms, ls, rates, expand, w, res, tm=512):
    t, n = res.shape
    tm = min(tm, t)
    row = pl.BlockSpec((tm, n), lambda i: (i, 0))

    def by_residue(a, rate):
        width = a.shape[1]
        return a.reshape(rate, t // rate, width), pl.BlockSpec((rate, tm // rate, width), lambda i: (0, i, 0))

    branch = [by_residue(a, r) for group in (os_, ms, ls) for a, r in zip(group, rates)]
    return pl.pallas_call(
        functools.partial(_dil_out_kernel, tm=tm, rates=tuple(rates)),
        out_shape=jax.ShapeDtypeStruct((t, n), F32),
        grid=(t // tm,),
        in_specs=[spec for _, spec in branch] + [
            pl.BlockSpec((LANES, n), lambda i: (0, 0)),
            pl.BlockSpec((n, n), lambda i: (0, 0)),
            row,
        ],
        out_specs=row,
        scratch_shapes=[pltpu.VMEM((n // LANES, tm, LANES), F32), pltpu.VMEM((1, tm, LANES), F32)],
        compiler_params=_cparams("arbitrary"),
        name="dil_out",
    )(*[a for a, _ in branch], expand, w, res)


def _cumsum_kernel(x_ref, c_ref, ct_ref, carry_sc, *, tb):
    @pl.when(pl.program_id(0) == 0)
    def _():
        carry_sc[...] = jnp.zeros(carry_sc.shape, F32)

    x = x_ref[...]
    row = lax.broadcasted_iota(jnp.int32, (tb, tb), 0)
    colm = lax.broadcasted_iota(jnp.int32, (tb, tb), 1)
    tri = jnp.where(colm <= row, 1.0, 0.0).astype(BF16)
    hi = x.astype(BF16)
    r1 = x - hi.astype(F32)
    mid = r1.astype(BF16)
    lo = (r1 - mid.astype(F32)).astype(BF16)
    c = carry_sc[...] + (_dot(tri, hi) + _dot(tri, mid) + _dot(tri, lo))
    c2 = c * LOG2E
    c_ref[...] = c2
    ct_ref[...] = c2.T
    carry_sc[...] = c[tb - 1:tb, :]


def _cumsum(x, tb=256):
    t = x.shape[0]
    tb = min(tb, t)
    return pl.pallas_call(
        functools.partial(_cumsum_kernel, tb=tb),
        out_shape=(jax.ShapeDtypeStruct((t, LANES), F32), jax.ShapeDtypeStruct((LANES, t), F32)),
        grid=(t // tb,),
        in_specs=[pl.BlockSpec((tb, LANES), lambda i: (i, 0))],
        out_specs=(pl.BlockSpec((tb, LANES), lambda i: (i, 0)), pl.BlockSpec((LANES, tb), lambda i: (0, i))),
        scratch_shapes=[pltpu.VMEM((1, LANES), F32)],
        compiler_params=_cparams("arbitrary"),
        name="fox_cumsum",
    )(x)


def _ffn_kernel(h_ref, hp_ref, g_ref, wg_ref, wv_ref, cwg_ref, cwv_ref, cbg_ref, cbv_ref, wo_ref, gf_ref, o_ref, a_sc,
                *, tm, final):
    i = pl.program_id(0)
    j = pl.program_id(1)

    @pl.when(j == 0)
    def _():
        g = g_ref[...]
        halo = _rms(hp_ref[...], g)
        a_sc[0:CONV_HALO, :] = jnp.where(i == 0, 0.0, halo).astype(BF16)
        a_sc[CONV_HALO:, :] = _rms(h_ref[...], g).astype(BF16)
        o_ref[...] = h_ref[...]

    a = a_sc[...]

    def conv(w_ref, cw_ref, cb_ref, lo, hi):
        u = _dot(a, w_ref[:, lo:hi])
        cw = cw_ref[:, lo:hi]
        c = cb_ref[:, lo:hi] + pltpu.roll(u, 2, 0)[CONV_HALO:] * cw[0:1]
        c = c + pltpu.roll(u, 1, 0)[CONV_HALO:] * cw[1:2]
        return c + u[CONV_HALO:] * cw[2:3]

    tf = wo_ref.shape[0]
    contrib = None
    for lo in range(0, tf, FFN_SLAB):
        hi = lo + FFN_SLAB
        gate = conv(wg_ref, cwg_ref, cbg_ref, lo, hi)
        val = conv(wv_ref, cwv_ref, cbv_ref, lo, hi)
        act = (gate / (1.0 + jnp.exp(-gate))) * val
        part = _dot(act.astype(BF16), wo_ref[lo:hi, :])
        contrib = part if contrib is None else contrib + part
    o_ref[...] += contrib

    if final:
        @pl.when(j == pl.num_programs(1) - 1)
        def _():
            o_ref[...] = _rms(o_ref[...], gf_ref[...])


def _ffn(h, gain, layer, w_in, conv_w, conv_b, w_out, final_gain, final, tm=1024, tf=512):
    t, dm = h.shape
    ff = w_out.shape[1]
    tm = min(tm, t)
    nf = ff // tf
    halo_blocks = tm // CONV_HALO
    once = pl.Buffered(1)
    return pl.pallas_call(
        functools.partial(_ffn_kernel, tm=tm, final=final),
        out_shape=jax.ShapeDtypeStruct((t, dm), F32),
        grid=(t // tm, nf),
        in_specs=[
            pl.BlockSpec((tm, dm), lambda i, j: (i, 0), pipeline_mode=once),
            pl.BlockSpec((CONV_HALO, dm), lambda i, j: (jnp.maximum(i * halo_blocks - 1, 0), 0)),
            pl.BlockSpec((1, dm), lambda i, j: (0, 0)),
            pl.BlockSpec((None, dm, tf), lambda i, j: (layer, 0, j)),
            pl.BlockSpec((None, dm, tf), lambda i, j: (layer, 0, j + nf)),
            pl.BlockSpec((None, CONV_WIDTH, tf), lambda i, j: (layer, 0, j)),
            pl.BlockSpec((None, CONV_WIDTH, tf), lambda i, j: (layer, 0, j + nf)),
            pl.BlockSpec((None, 1, tf), lambda i, j: (layer, 0, j)),
            pl.BlockSpec((None, 1, tf), lambda i, j: (layer, 0, j + nf)),
            pl.BlockSpec((None, tf, dm), lambda i, j: (layer, j, 0)),
            pl.BlockSpec((1, dm), lambda i, j: (0, 0)),
        ],
        out_specs=pl.BlockSpec((tm, dm), lambda i, j: (i, 0)),
        scratch_shapes=[pltpu.VMEM((tm + CONV_HALO, dm), BF16)],
        compiler_params=_cparams("arbitrary", "arbitrary"),
        name="conv_ffn",
    )(h, h, gain.reshape(1, dm), w_in, w_in, conv_w, conv_w, conv_b[:, None, :], conv_b[:, None, :], w_out,
      final_gain.reshape(1, dm))


def _mla(h, gain, positions, w_in, g_q, g_kv, w_qb, w_kvb, w_o, tm=1024):
    t, dm = h.shape
    tm = min(tm, t)
    hh = MLA_HEADS
    qk = MLA_NOPE + MLA_ROPE
    lat_w = MLA_Q_RANK + MLA_KV_RANK + LANES
    cos_t, sin_t = _rope_tables(positions)
    w_in_p = jnp.pad(w_in, ((0, 0), (0, lat_w - w_in.shape[1]))).astype(BF16)
    w_qb_p = jnp.pad(w_qb.reshape(MLA_Q_RANK, hh, qk), ((0, 0), (0, 0), (0, 2 * LANES - qk)))
    w_qb_p = w_qb_p.reshape(MLA_Q_RANK, hh * 2 * LANES).astype(BF16)

    row128 = pl.BlockSpec((tm, LANES), lambda i, j: (i, 0))

    def lat_epilogue(y, extras, outs):
        gq_ref, gkv_ref, c_ref, s_ref = extras
        cq_ref, ckv_ref, kr_ref = outs
        cq_ref[...] = _rms(y[:, :MLA_Q_RANK], gq_ref[...]).astype(BF16)
        ckv_ref[...] = _rms(y[:, MLA_Q_RANK:MLA_Q_RANK + MLA_KV_RANK], gkv_ref[...]).astype(BF16)
        kr_ref[...] = _rope128(y[:, MLA_Q_RANK + MLA_KV_RANK:], c_ref[...], s_ref[...]).astype(BF16)

    rank_row = pl.BlockSpec((tm, MLA_Q_RANK), lambda i, j: (i, 0))
    c_q, c_kv, k_rope = _proj(
        "mla_latents", h, gain, w_in_p,
        extras=[(g_q.reshape(1, -1), pl.BlockSpec((1, MLA_Q_RANK), lambda i, j: (0, 0))),
                (g_kv.reshape(1, -1), pl.BlockSpec((1, MLA_KV_RANK), lambda i, j: (0, 0))),
                (cos_t, row128), (sin_t, row128)],
        outs=[((t, MLA_Q_RANK), BF16, rank_row), ((t, MLA_KV_RANK), BF16, rank_row), ((t, LANES), BF16, row128)],
        epilogue=lat_epilogue, tm=tm, tn=lat_w)

    heads_per_tile = 4
    tn = heads_per_tile * 2 * LANES
    scale = qk ** -0.5 * LOG2E

    def q_epilogue(y, extras, outs):
        c_ref, s_ref = extras
        (q_ref,) = outs
        y = y * scale
        for a in range(heads_per_tile):
            base = a * 2 * LANES
            q_ref[a, :, :LANES] = y[:, base:base + LANES].astype(BF16)
            q_ref[a, :, LANES:] = _rope128(y[:, base + LANES:base + 2 * LANES], c_ref[...], s_ref[...]).astype(BF16)

    head_blk = pl.BlockSpec((heads_per_tile, tm, 2 * LANES), lambda i, j: (j, i, 0))
    (q_cat,) = _proj(
        "mla_q", c_q, None, w_qb_p,
        extras=[(cos_t, row128), (sin_t, row128)],
        outs=[((hh, t, 2 * LANES), BF16, head_blk)],
        epilogue=q_epilogue, tm=tm, tn=tn)

    def kv_epilogue(y, extras, outs):
        (kr_ref,) = extras
        k_ref, v_ref = outs
        for a in range(heads_per_tile):
            base = a * 2 * LANES
            k_ref[a, :, :LANES] = y[:, base:base + LANES].astype(BF16)
            k_ref[a, :, LANES:] = kr_ref[...]
            v_ref[a, :, :LANES] = y[:, base + LANES:base + 2 * LANES].astype(BF16)
            v_ref[a, :, LANES:] = jnp.ones((tm, LANES), BF16)

    k_cat, v = _proj(
        "mla_kv", c_kv, None, w_kvb.astype(BF16),
        extras=[(k_rope, row128)],
        outs=[((hh, t, 2 * LANES), BF16, head_blk), ((hh, t, 2 * LANES), BF16, head_blk)],
        epilogue=kv_epilogue, tm=tm, tn=tn)

    o = _flash("mla_attn", q_cat, k_cat, v, 0, 0, 0, hh)
    return _out_proj("mla_out", o, w_o.astype(BF16), jnp.zeros((1, dm), F32), h)


def _cols_epilogue(y, extras, outs):
    b_ref, sc_ref = extras
    (o_ref,) = outs
    o_ref[...] = ((y + b_ref[...]) * sc_ref[...]).astype(BF16)


def _swa(h, gain, rel_bias, w_qkv, b_qkv, sinks, w_o, b_o, tm=1024, tn=512):
    t, dm = h.shape
    tm = min(tm, t)
    n = w_qkv.shape[1]
    nq = SWA_Q_HEADS * SWA_HEAD_DIM
    grp = SWA_Q_HEADS // SWA_KV_HEADS
    colscale = jnp.concatenate([jnp.full((nq,), SWA_HEAD_DIM ** -0.5 * LOG2E, F32), jnp.ones((n - nq,), F32)])[None]
    vec = pl.BlockSpec((1, tn), lambda i, j: (0, j))
    (qkv,) = _proj(
        "swa_qkv", h, gain, w_qkv.astype(BF16),
        extras=[(b_qkv.reshape(1, n), vec), (colscale, vec)],
        outs=[((t, n), BF16, pl.BlockSpec((tm, tn), lambda i, j: (i, j)))],
        epilogue=_cols_epilogue, tm=tm, tn=tn)
    tab = _band_table(rel_bias, 1, SWA_WINDOW - 1)
    sink_rep = jnp.repeat(sinks.astype(F32) * LOG2E, SWA_HEAD_DIM)[None]
    k0 = nq // LANES
    v0 = k0 + SWA_KV_HEADS * SWA_HEAD_DIM // LANES
    (o,) = _band("swa_attn", qkv, lambda j: j, lambda j: k0 + j, lambda j: v0 + j, SWA_KV_HEADS // 2, 1, grp,
                 tab, lambda j: j, sink_rep=sink_rep, tile=BLK)
    return _out_proj("swa_out", o, w_o.astype(BF16), b_o.reshape(1, dm), h)


def _dilated(h, gain, rel_bias, w_qkv, w_o, tm=1024, tn=512):
    t, dm = h.shape
    tm = min(tm, t)
    n = w_qkv.shape[1]
    nq = DIL_HEADS * DIL_HEAD_DIM
    colscale = jnp.concatenate([jnp.full((nq,), DIL_HEAD_DIM ** -0.5 * LOG2E, F32), jnp.ones((n - nq,), F32)])[None]
    vec = pl.BlockSpec((1, tn), lambda i, j: (0, j))
    rates = tuple(rate for _, rate in DIL_PAIRS)

    def qkv_epilogue(y, extras, outs):
        (sc_ref,) = extras
        stage = outs[-1]
        y = y * sc_ref[...]
        for k in range(tn // LANES):
            stage[k] = y[:, k * LANES:(k + 1) * LANES]
        for o_ref, rate in zip(outs[:-1], rates):
            if rate == 1:
                o_ref[0] = y.astype(BF16)
            else:
                for c in range(rate):
                    for k in range(tn // LANES):
                        rows = stage[k, pl.ds(c, tm // rate, stride=rate), :]
                        o_ref[c, :, k * LANES:(k + 1) * LANES] = rows.astype(BF16)

    qkvs = _proj(
        "dil_qkv", h, gain, w_qkv.astype(BF16),
        extras=[(colscale, vec)],
        outs=[((rate, t // rate, n), BF16, pl.BlockSpec((rate, tm // rate, tn), lambda i, j: (0, i, j)))
              for rate in rates],
        epilogue=qkv_epilogue, tm=tm, tn=tn, scratch=[pltpu.VMEM((tn // LANES, tm, LANES), F32)])
    npairs = 2
    groups = DIL_HEADS // (2 * npairs)
    os_, ms, ls = [], [], []
    for (window, rate), qkv in zip(DIL_PAIRS, qkvs):
        n_keys = window // rate
        tab = _band_table(rel_bias, rate, n_keys)
        o, m, l = _band(
            f"dil_attn_r{rate}", qkv.reshape(t, n),
            lambda j: j, lambda j: groups + j, lambda j: 2 * groups + j,
            groups, npairs, 1, tab, lambda j: j, seq_len=t // rate)
        os_.append(o)
        ms.append(m)
        ls.append(l)
    expand = jnp.repeat(jnp.eye(LANES, DIL_HEADS, dtype=BF16), DIL_HEAD_DIM, axis=1)
    return _dil_out(os_, ms, ls, rates, expand, w_o.astype(BF16), h)


def _fox(h, gain, w_in, b_f, w_o, tm=1024, tn=512):
    t, dm = h.shape
    tm = min(tm, t)
    hh = FOX_HEADS
    hd = hh * FOX_HEAD_DIM
    heads_per_tile = tn // FOX_HEAD_DIM
    colscale = jnp.concatenate([jnp.full((hd,), FOX_HEAD_DIM ** -0.5 * LOG2E, F32), jnp.ones((hd,), F32)])[None]
    w_all = jnp.pad(w_in, ((0, 0), (0, 3 * hd + LANES - w_in.shape[1]))).astype(BF16)

    def qk_epilogue(y, extras, outs):
        (sc_ref,) = extras
        (o_ref,) = outs
        y = y * sc_ref[...]
        for a in range(heads_per_tile):
            o_ref[a] = y[:, a * FOX_HEAD_DIM:(a + 1) * FOX_HEAD_DIM].astype(BF16)

    (qk,) = _proj(
        "fox_qk", h, gain, w_all,
        extras=[(colscale, pl.BlockSpec((1, tn), lambda i, j: (0, j)))],
        outs=[((2 * hh, t, FOX_HEAD_DIM), BF16,
               pl.BlockSpec((heads_per_tile, tm, FOX_HEAD_DIM), lambda i, j: (j, i, 0)))],
        epilogue=qk_epilogue, tm=tm, tn=tn, col0=0, n=2 * hd)

    def v_epilogue(y, extras, outs):
        (o_ref,) = outs
        for a in range(heads_per_tile):
            o_ref[a, :, :LANES] = y[:, a * FOX_HEAD_DIM:(a + 1) * FOX_HEAD_DIM].astype(BF16)
            o_ref[a, :, LANES:] = jnp.ones((tm, LANES), BF16)

    (v_aug,) = _proj(
        "fox_v", h, gain, w_all,
        extras=[],
        outs=[((hh, t, 2 * LANES), BF16, pl.BlockSpec((heads_per_tile, tm, 2 * LANES), lambda i, j: (j, i, 0)))],
        epilogue=v_epilogue, tm=tm, tn=tn, col0=2 * hd, n=hd)

    def gate_epilogue(y, extras, outs):
        (b_ref,) = extras
        (o_ref,) = outs
        x = y + b_ref[...]
        o_ref[...] = jnp.minimum(x, 0.0) - jnp.log(1.0 + jnp.exp(-jnp.abs(x)))

    b_gate = jnp.pad(b_f.astype(F32), (0, LANES - hh))[None]
    (log_f,) = _proj(
        "fox_gate", h, gain, w_all,
        extras=[(b_gate, pl.BlockSpec((1, LANES), lambda i, j: (0, 0)))],
        outs=[((t, LANES), F32, pl.BlockSpec((tm, LANES), lambda i, j: (i, 0)))],
        epilogue=gate_epilogue, tm=tm, tn=LANES, col0=3 * hd, n=LANES)
    cum, cum_t = _cumsum(log_f)
    o = _flash("fox_attn", qk, qk, v_aug, 0, hh, 0, hh, cum=cum, cumt=cum_t)
    return _out_proj("fox_out", o, w_o.astype(BF16), jnp.zeros((1, dm), F32), h)


def kernel(x, positions, rel_bias, norm_mix, norm_ffn, mla_w_in, mla_g_q, mla_g_kv, mla_w_qb, mla_w_kvb, mla_w_o, swa_w_qkv, swa_b_qkv, swa_sinks, swa_w_o, swa_b_o, dil_w_qkv, dil_w_o, fox_w_in, fox_b_f, fox_w_o, ffn_w_in, ffn_conv_w, ffn_conv_b, ffn_w_out, final_norm):
    bsz, t, dm = x.shape
    assert bsz == 1
    depth = norm_mix.shape[0]
    h = x.reshape(t, dm)
    pos = positions.reshape(t)
    ffn_w_in_b = ffn_w_in.astype(BF16)
    ffn_w_out_b = ffn_w_out.astype(BF16)
    for i in range(depth):
        kind = i % 4
        r = i // 4
        if kind == 0:
            h = _mla(h, norm_mix[i], pos, mla_w_in[r], mla_g_q[r], mla_g_kv[r], mla_w_qb[r], mla_w_kvb[r], mla_w_o[r])
        elif kind == 1:
            h = _swa(h, norm_mix[i], rel_bias, swa_w_qkv[r], swa_b_qkv[r], swa_sinks[r], swa_w_o[r], swa_b_o[r])
        elif kind == 2:
            h = _dilated(h, norm_mix[i], rel_bias, dil_w_qkv[r], dil_w_o[r])
        else:
            h = _fox(h, norm_mix[i], fox_w_in[r], fox_b_f[r], fox_w_o[r])
        h = _ffn(h, norm_ffn[i], i, ffn_w_in_b, ffn_conv_w, ffn_conv_b, ffn_w_out_b, final_norm,
                 final=(i == depth - 1))
    return h.reshape(bsz, t, dm)
```

```python
import functools
import math

import jax
import jax.numpy as jnp
from jax import lax
from jax.experimental import pallas as pl
from jax.experimental.pallas import tpu as pltpu

F32 = jnp.float32
BF16 = jnp.bfloat16
EPS = 1e-6
NEG = -1e30
LOG2E = math.log2(math.e)

LANES = 128
VMEM_LIMIT_BYTES = 56 * 1024 * 1024
BLK = 128

ROPE_THETA = 10000.0
REL_BUCKETS = 32
REL_MAX_DIST = 2048
MLA_HEADS, MLA_Q_RANK, MLA_KV_RANK, MLA_NOPE, MLA_ROPE, MLA_V = 16, 512, 512, 128, 64, 128
SWA_Q_HEADS, SWA_KV_HEADS, SWA_HEAD_DIM, SWA_WINDOW = 32, 4, 64, 128
DIL_HEADS, DIL_HEAD_DIM = 32, 64
DIL_PAIRS = ((128, 1), (512, 4), (2048, 16))
FOX_HEADS, FOX_HEAD_DIM = 16, 128
CONV_WIDTH = 3
CONV_HALO = 16
FFN_SLAB = 512
FLASH_HEADS = 1


def _cparams(*sem):
    return pltpu.CompilerParams(dimension_semantics=sem, vmem_limit_bytes=VMEM_LIMIT_BYTES)


def _rms(x, g):
    return x * lax.rsqrt(jnp.mean(x * x, axis=-1, keepdims=True) + EPS) * g


def _dot(a, b):
    return jnp.dot(a, b, preferred_element_type=F32)


def _dot_nt(a, b):
    return lax.dot_general(a, b, (((1,), (1,)), ((), ())), preferred_element_type=F32)


def _rope_tab_kernel(pos_ref, f_ref, sgn_ref, c_ref, s_ref):
    ang = pos_ref[...].astype(F32) * f_ref[...]
    c_ref[...] = jnp.cos(ang)
    s_ref[...] = jnp.sin(ang) * sgn_ref[...]


def _rope_tables(positions):
    t = positions.shape[0]
    half = MLA_ROPE // 2
    inv_freq = ROPE_THETA ** (-jnp.arange(half, dtype=F32) / half)
    zeros = jnp.zeros((LANES - 2 * half,), F32)
    freq = jnp.concatenate([inv_freq, inv_freq, zeros])[None]
    sgn = jnp.concatenate([-jnp.ones((half,), F32), jnp.ones((half,), F32), zeros])[None]
    tm = min(t, 1024)
    row = pl.BlockSpec((tm, LANES), lambda i: (i, 0))
    vec = pl.BlockSpec((1, LANES), lambda i: (0, 0))
    return pl.pallas_call(
        _rope_tab_kernel,
        out_shape=(jax.ShapeDtypeStruct((t, LANES), F32),) * 2,
        grid=(t // tm,),
        in_specs=[pl.BlockSpec((tm, 1), lambda i: (i, 0)), vec, vec],
        out_specs=(row, row),
        compiler_params=_cparams("arbitrary"),
        name="rope_tables",
    )(positions.reshape(t, 1), freq, sgn)


def _rope128(z, c, s):
    half = MLA_ROPE // 2
    lane = lax.broadcasted_iota(jnp.int32, z.shape, 1)
    swapped = jnp.where(lane < half, pltpu.roll(z, LANES - half, 1), pltpu.roll(z, half, 1))
    return z * c + swapped * s


def _proj_kernel(*refs, norm, n_extra, epilogue):
    if norm:
        lhs_ref, g_ref, w_ref = refs[:3]
        rest = refs[3:]
        a_ref = rest[-1]
        rest = rest[:-1]

        @pl.when(pl.program_id(1) == 0)
        def _():
            a_ref[...] = _rms(lhs_ref[...], g_ref[...]).astype(BF16)

        a = a_ref[...]
    else:
        lhs_ref, w_ref = refs[:2]
        rest = refs[2:]
        a = lhs_ref[...]
    y = _dot(a, w_ref[...])
    epilogue(y, rest[:n_extra], rest[n_extra:])


def _proj(name, lhs, gain, w, extras, outs, epilogue, tm, tn, col0=0, n=None, scratch=()):
    m, k = lhs.shape
    n = w.shape[1] if n is None else n
    woff = col0 // tn
    norm = gain is not None
    in_specs = [pl.BlockSpec((tm, k), lambda i, j: (i, 0))]
    args = [lhs]
    if norm:
        in_specs.append(pl.BlockSpec((1, k), lambda i, j: (0, 0)))
        args.append(gain.reshape(1, k))
    in_specs.append(pl.BlockSpec((k, tn), lambda i, j: (0, woff + j)))
    args.append(w)
    for arr, spec in extras:
        in_specs.append(spec)
        args.append(arr)
    res = pl.pallas_call(
        functools.partial(_proj_kernel, norm=norm, n_extra=len(extras), epilogue=epilogue),
        out_shape=tuple(jax.ShapeDtypeStruct(s, d) for s, d, _ in outs),
        grid=(m // tm, n // tn),
        in_specs=in_specs,
        out_specs=tuple(spec for _, _, spec in outs),
        scratch_shapes=list(scratch) + ([pltpu.VMEM((tm, k), BF16)] if norm else []),
        compiler_params=_cparams("arbitrary", "arbitrary"),
        name=name,
    )(*args)
    return res


def _out_proj_kernel(lhs_ref, w_ref, b_ref, res_ref, o_ref):
    o_ref[...] = res_ref[...] + (_dot(lhs_ref[...], w_ref[...]) + b_ref[...])


def _out_proj(name, lhs, w, bias, res, tm=512):
    m, k = lhs.shape
    n = w.shape[1]
    tm = min(tm, m)
    return pl.pallas_call(
        _out_proj_kernel,
        out_shape=jax.ShapeDtypeStruct((m, n), F32),
        grid=(m // tm,),
        in_specs=[
            pl.BlockSpec((tm, k), lambda i: (i, 0)),
            pl.BlockSpec((k, n), lambda i: (0, 0)),
            pl.BlockSpec((1, n), lambda i: (0, 0)),
            pl.BlockSpec((tm, n), lambda i: (i, 0)),
        ],
        out_specs=pl.BlockSpec((tm, n), lambda i: (i, 0)),
        compiler_params=_cparams("arbitrary"),
        name=name,
    )(lhs, w, bias, res)


def _flash_kernel(*refs, bq, fox):
    if fox:
        q_ref, k_ref, v_ref, cum_ref, cumt_ref, o_ref, s_sc, m_sc, acc_sc = refs
    else:
        q_ref, k_ref, v_ref, o_ref, s_sc, m_sc, acc_sc = refs
    nh = q_ref.shape[0]
    g = pl.program_id(0)
    i = pl.program_id(1)
    reps = bq // LANES
    m_sc[...] = jnp.full(m_sc.shape, NEG, F32)
    acc_sc[...] = jnp.zeros(acc_sc.shape, F32)
    q = [q_ref[a] for a in range(nh)]
    if fox:
        lane = lax.broadcasted_iota(jnp.int32, (bq, LANES), 1)
        cq = [jnp.broadcast_to(jnp.sum(jnp.where(lane == nh * g + a, cum_ref[...], 0.0), axis=1, keepdims=True),
                               (bq, LANES)) for a in range(nh)]

    def scores(c, slot):
        k0 = pl.multiple_of(c * bq, bq)
        for a in range(nh):
            s_sc[2 * a + slot] = _dot_nt(q[a], k_ref[a, pl.ds(k0, bq), :])

    def softmax_pv(c, slot, masked):
        k0 = pl.multiple_of(c * bq, bq)
        for a in range(nh):
            t = s_sc[2 * a + slot]
            if fox:
                t = t - cumt_ref[pl.ds(nh * g + a, 1), pl.ds(k0, bq)]
            if masked:
                row = lax.broadcasted_iota(jnp.int32, (bq, bq), 0)
                col = lax.broadcasted_iota(jnp.int32, (bq, bq), 1)
                t = jnp.where(col <= row, t, NEG)
            r = jnp.max(t, axis=1, keepdims=True)
            m_prev = m_sc[a]
            m_new = jnp.maximum(m_prev, r + cq[a]) if fox else jnp.maximum(m_prev, r)
            alpha = jnp.exp2(m_prev - m_new)
            shift = m_new - cq[a] if fox else m_new
            p = jnp.exp2(t - jnp.tile(shift, (1, reps)))
            pv = _dot(p.astype(BF16), v_ref[a, pl.ds(k0, bq), :])
            acc_sc[a] = jnp.tile(alpha, (1, 2)) * acc_sc[a] + pv
            m_sc[a] = m_new

    scores(0, 0)

    def pair(c):
        scores(c + 1, 1)
        softmax_pv(c, 0, False)
        scores(c + 2, 0)
        softmax_pv(c + 1, 1, False)

    def body(u, carry):
        for a in range(4):
            pair(8 * u + 2 * a)
        return carry

    n_octs = lax.shift_right_logical(i, 3)
    lax.fori_loop(0, n_octs, body, 0)
    done = 8 * n_octs

    @pl.when(lax.bitwise_and(i, 4) == 4)
    def _():
        pair(done)
        pair(done + 2)

    done = done + lax.bitwise_and(i, 4)

    @pl.when(lax.bitwise_and(i, 2) == 2)
    def _():
        pair(done)

    @pl.when(lax.bitwise_and(i, 1) == 1)
    def _():
        scores(i, 1)
        softmax_pv(i - 1, 0, False)
        softmax_pv(i, 1, True)

    @pl.when(lax.bitwise_and(i, 1) == 0)
    def _():
        softmax_pv(i, 0, True)

    for a in range(nh):
        acc = acc_sc[a]
        o_ref[:, a * LANES:(a + 1) * LANES] = (acc[:, :LANES] / acc[:, LANES:]).astype(BF16)


def _flash(name, q, k, v, q_off, k_off, v_off, heads, cum=None, cumt=None, bq=512, nh=FLASH_HEADS):
    t = q.shape[1]
    dk = q.shape[2]
    dv = v.shape[2]
    assert dv == 2 * LANES
    assert heads % nh == 0 and q_off % nh == 0 and k_off % nh == 0 and v_off % nh == 0
    bq = min(bq, t)
    fox = cum is not None
    in_specs = [
        pl.BlockSpec((nh, bq, dk), lambda h, i: (q_off // nh + h, i, 0)),
        pl.BlockSpec((nh, t, dk), lambda h, i: (k_off // nh + h, 0, 0)),
        pl.BlockSpec((nh, t, dv), lambda h, i: (v_off // nh + h, 0, 0)),
    ]
    args = [q, k, v]
    if fox:
        in_specs += [
            pl.BlockSpec((bq, LANES), lambda h, i: (i, 0)),
            pl.BlockSpec(cumt.shape, lambda h, i: (0, 0)),
        ]
        args += [cum, cumt]
    return pl.pallas_call(
        functools.partial(_flash_kernel, bq=bq, fox=fox),
        out_shape=jax.ShapeDtypeStruct((t, heads * LANES), BF16),
        grid=(heads // nh, t // bq),
        in_specs=in_specs,
        out_specs=pl.BlockSpec((bq, nh * LANES), lambda h, i: (i, h)),
        scratch_shapes=[pltpu.VMEM((2 * nh, bq, bq), F32), pltpu.VMEM((nh, bq, LANES), F32),
                        pltpu.VMEM((nh, bq, dv), F32)],
        compiler_params=_cparams("arbitrary", "arbitrary"),
        name=name,
    )(*args)


def _band_kernel(*refs, nbk, npairs, grp, sinks, tiles_per_seq):
    if sinks:
        q_ref, k_ref, v_ref, kp_ref, vp_ref, tab_ref, sink_ref, o_ref, kf_sc, vf_sc = refs
    else:
        q_ref, k_ref, v_ref, kp_ref, vp_ref, tab_ref, o_ref, m_ref, l_ref, kf_sc, vf_sc = refs
    i = pl.program_id(0)
    j = pl.program_id(1)
    kf_sc[0:BLK, :] = kp_ref[...]
    kf_sc[BLK:, :] = k_ref[...]
    vf_sc[0:BLK, :] = vp_ref[...]
    vf_sc[BLK:, :] = v_ref[...]
    d = LANES // 2
    n_kv = 2 * npairs
    col = lax.broadcasted_iota(jnp.int32, (1, 2 * BLK), 1)
    penrow = jnp.where(jnp.logical_and(col < BLK, i % tiles_per_seq == 0), NEG, 0.0)
    if not sinks:
        stat_lane = lax.broadcasted_iota(jnp.int32, (BLK, LANES), 1)
        steps_per_stat = LANES // n_kv

        @pl.when(j % steps_per_stat == 0)
        def _():
            m_ref[...] = jnp.zeros(m_ref.shape, F32)
            l_ref[...] = jnp.zeros(l_ref.shape, F32)

    lo_k = lax.broadcasted_iota(jnp.int32, (2 * BLK, LANES), 1) < d
    lo_o = lax.broadcasted_iota(jnp.int32, (BLK, LANES), 1) < d
    swap = lambda a: jnp.concatenate([a[:, d:], a[:, :d]], axis=1)

    for b in range(nbk):
        r0 = b * BLK
        if not sinks:
            m_st = m_ref[r0:r0 + BLK, :]
            l_st = l_ref[r0:r0 + BLK, :]
        for kp in range(npairs):
            kk = kf_sc[r0:r0 + 2 * BLK, kp * LANES:(kp + 1) * LANES]
            vv = vf_sc[r0:r0 + 2 * BLK, kp * LANES:(kp + 1) * LANES]
            zero = jnp.zeros_like(kk)
            k_own = [jnp.where(lo_k, kk, zero), jnp.where(lo_k, zero, kk)]
            if grp > 1:
                k_by = [[k_own[0], swap(k_own[0])], [swap(k_own[1]), k_own[1]]]
                v_by = [[vv, swap(vv)], [swap(vv), vv]]
            else:
                k_by = [[k_own[0], None], [None, k_own[1]]]
                v_by = [[vv, None], [None, vv]]
            for c in range(grp):
                qcol = kp * grp + c
                qq = q_ref[r0:r0 + BLK, qcol * LANES:(qcol + 1) * LANES]
                halves = []
                for qh in range(2):
                    hq = 2 * qcol + qh
                    kh = hq // grp - 2 * kp
                    s = _dot_nt(qq, k_by[kh][qh]) + tab_ref[hq]
                    if b == 0:
                        s = s + penrow
                    m = jnp.max(s, axis=1, keepdims=True)
                    p = jnp.exp2(s - m)
                    l = jnp.sum(p, axis=1, keepdims=True)
                    acc = _dot(p.astype(BF16), v_by[kh][qh])
                    if sinks:
                        sink = sink_ref[:, qcol * LANES:(qcol + 1) * LANES]
                        m2 = jnp.maximum(m, sink)
                        a = jnp.exp2(m - m2)
                        den = l * a + jnp.exp2(sink - m2)
                        halves.append(acc * (a / den))
                    else:
                        halves.append(acc / l)
                        sel = stat_lane == (n_kv * j + hq) % LANES
                        m_st = jnp.where(sel, m, m_st)
                        l_st = jnp.where(sel, l, l_st)
                o_ref[r0:r0 + BLK, qcol * LANES:(qcol + 1) * LANES] = jnp.where(lo_o, halves[0], halves[1]).astype(BF16)
        if not sinks:
            m_ref[r0:r0 + BLK, :] = m_st
            l_ref[r0:r0 + BLK, :] = l_st


def _band(name, x, q_blk, k_blk, v_blk, n_steps, npairs, grp, tab, tab_blk, sink_rep=None, tile=512, seq_len=None):
    length = x.shape[0]
    seq_len = length if seq_len is None else seq_len
    tb = min(tile, seq_len)
    nbk = tb // BLK
    kw = LANES * npairs
    qw = kw * grp
    n_kv = 2 * npairs
    n_out = n_steps * qw
    sinks = sink_rep is not None
    prev = lambda i, j: jnp.maximum(i * nbk - 1, 0)
    in_specs = [
        pl.BlockSpec((tb, qw), lambda i, j: (i, q_blk(j))),
        pl.BlockSpec((tb, kw), lambda i, j: (i, k_blk(j))),
        pl.BlockSpec((tb, kw), lambda i, j: (i, v_blk(j))),
        pl.BlockSpec((BLK, kw), lambda i, j: (prev(i, j), k_blk(j))),
        pl.BlockSpec((BLK, kw), lambda i, j: (prev(i, j), v_blk(j))),
        pl.BlockSpec((n_kv * grp, BLK, 2 * BLK), lambda i, j: (tab_blk(j), 0, 0)),
    ]
    args = [x, x, x, x, x, tab]
    out_shape = [jax.ShapeDtypeStruct((length, n_out), BF16)]
    out_specs = [pl.BlockSpec((tb, qw), lambda i, j: (i, j))]
    if sinks:
        in_specs.append(pl.BlockSpec((1, qw), lambda i, j: (0, j)))
        args.append(sink_rep)
    else:
        n_stat = max(LANES, n_kv * n_steps)
        stat = pl.BlockSpec((tb, LANES), lambda i, j: (i, j // (LANES // n_kv)))
        out_shape += [jax.ShapeDtypeStruct((length, n_stat), F32)] * 2
        out_specs += [stat, stat]
    return pl.pallas_call(
        functools.partial(_band_kernel, nbk=nbk, npairs=npairs, grp=grp, sinks=sinks, tiles_per_seq=seq_len // tb),
        out_shape=tuple(out_shape),
        grid=(length // tb, n_steps),
        in_specs=in_specs,
        out_specs=tuple(out_specs),
        scratch_shapes=[pltpu.VMEM((tb + BLK, kw), BF16), pltpu.VMEM((tb + BLK, kw), BF16)],
        compiler_params=_cparams("arbitrary", "arbitrary"),
        name=name,
    )(*args)


def _t5_bucket(n):
    exact = REL_BUCKETS // 2
    nf = jnp.maximum(n, 1).astype(F32)
    large = exact + (jnp.log(nf / exact) / math.log(REL_MAX_DIST / exact) * (REL_BUCKETS - exact)).astype(jnp.int32)
    return jnp.where(n < exact, n, jnp.minimum(large, REL_BUCKETS - 1))


def _band_table(rel_bias, rate, max_dist):
    nh = rel_bias.shape[1]
    w = 2 * BLK
    dist = jnp.arange(w)
    row = rel_bias[_t5_bucket(rate * jnp.minimum(dist, max_dist))].T * LOG2E
    f = jnp.where(dist[None] <= max_dist, row, NEG).astype(F32)
    r = jnp.roll(f[:, ::-1], BLK + 1, axis=1)
    rr = jnp.concatenate([r, r], axis=1)
    flat = jnp.broadcast_to(rr[:, None, :], (nh, BLK + 1, 2 * w)).reshape(nh, (BLK + 1) * 2 * w)
    return flat[:, w:w + BLK * (2 * w - 1)].reshape(nh, BLK, 2 * w - 1)[:, :, :w]


def _dil_out_kernel(o1, o2, o3, m1, m2, m3, l1, l2, l3, e_ref, w_ref, res_ref, out_ref, row_sc, stat_sc, *, tm, rates):
    def in_order(ref, rate, sc):
        if rate == 1:
            return ref[0].astype(F32)
        groups = ref.shape[2] // LANES
        for c in range(rate):
            rows = ref[c].astype(F32)
            for k in range(groups):
                sc[k, pl.ds(c, tm // rate, stride=rate), :] = rows[:, k * LANES:(k + 1) * LANES]
        return jnp.concatenate([sc[k] for k in range(groups)], axis=1)

    ms = [in_order(m, r, stat_sc) for m, r in zip((m1, m2, m3), rates)]
    ls = [in_order(l, r, stat_sc) for l, r in zip((l1, l2, l3), rates)]
    lane = lax.broadcasted_iota(jnp.int32, ms[0].shape, 1)
    mx = jnp.maximum(jnp.maximum(ms[0], ms[1]), ms[2])
    ws = [l * jnp.exp2(m - mx) for m, l in zip(ms, ls)]
    den = ws[0] + ws[1] + ws[2]
    e = e_ref[...]
    o = None
    for w, o_ref, r in zip(ws, (o1, o2, o3), rates):
        wn = jnp.where(lane < DIL_HEADS, w / den, 0.0)
        hi = wn.astype(BF16)
        lo = (wn - hi.astype(F32)).astype(BF16)
        term = in_order(o_ref, r, row_sc) * (_dot(hi, e) + _dot(lo, e))
        o = term if o is None else o + term
    out_ref[...] = res_ref[...] + _dot(o.astype(BF16), w_ref[...])


def _dil_out(os_, ms, ls, rates, expand, w, res, tm=512):
    t, n = res.shape
    tm = min(tm, t)
    row = pl.BlockSpec((tm, n), lambda i: (i, 0))

    def by_residue(a, rate):
        width = a.shape[1]
        return a.reshape(rate, t // rate, width), pl.BlockSpec((rate, tm // rate, width), lambda i: (0, i, 0))

    branch = [by_residue(a, r) for group in (os_, ms, ls) for a, r in zip(group, rates)]
    return pl.pallas_call(
        functools.partial(_dil_out_kernel, tm=tm, rates=tuple(rates)),
        out_shape=jax.ShapeDtypeStruct((t, n), F32),
        grid=(t // tm,),
        in_specs=[spec for _, spec in branch] + [
            pl.BlockSpec((LANES, n), lambda i: (0, 0)),
            pl.BlockSpec((n, n), lambda i: (0, 0)),
            row,
        ],
        out_specs=row,
        scratch_shapes=[pltpu.VMEM((n // LANES, tm, LANES), F32), pltpu.VMEM((1, tm, LANES), F32)],
        compiler_params=_cparams("arbitrary"),
        name="dil_out",
    )(*[a for a, _ in branch], expand, w, res)


def _cumsum_kernel(x_ref, c_ref, ct_ref, carry_sc, *, tb):
    @pl.when(pl.program_id(0) == 0)
    def _():
        carry_sc[...] = jnp.zeros(carry_sc.shape, F32)

    x = x_ref[...]
    row = lax.broadcasted_iota(jnp.int32, (tb, tb), 0)
    colm = lax.broadcasted_iota(jnp.int32, (tb, tb), 1)
    tri = jnp.where(colm <= row, 1.0, 0.0).astype(BF16)
    hi = x.astype(BF16)
    r1 = x - hi.astype(F32)
    mid = r1.astype(BF16)
    lo = (r1 - mid.astype(F32)).astype(BF16)
    c = carry_sc[...] + (_dot(tri, hi) + _dot(tri, mid) + _dot(tri, lo))
    c2 = c * LOG2E
    c_ref[...] = c2
    ct_ref[...] = c2.T
    carry_sc[...] = c[tb - 1:tb, :]


def _cumsum(x, tb=256):
    t = x.shape[0]
    tb = min(tb, t)
    return pl.pallas_call(
        functools.partial(_cumsum_kernel, tb=tb),
        out_shape=(jax.ShapeDtypeStruct((t, LANES), F32), jax.ShapeDtypeStruct((LANES, t), F32)),
        grid=(t // tb,),
        in_specs=[pl.BlockSpec((tb, LANES), lambda i: (i, 0))],
        out_specs=(pl.BlockSpec((tb, LANES), lambda i: (i, 0)), pl.BlockSpec((LANES, tb), lambda i: (0, i))),
        scratch_shapes=[pltpu.VMEM((1, LANES), F32)],
        compiler_params=_cparams("arbitrary"),
        name="fox_cumsum",
    )(x)


def _ffn_kernel(h_ref, hp_ref, g_ref, wg_ref, wv_ref, cwg_ref, cwv_ref, cbg_ref, cbv_ref, wo_ref, gf_ref, o_ref, a_sc,
                *, tm, final):
    i = pl.program_id(0)
    j = pl.program_id(1)

    @pl.when(j == 0)
    def _():
        g = g_ref[...]
        halo = _rms(hp_ref[...], g)
        a_sc[0:CONV_HALO, :] = jnp.where(i == 0, 0.0, halo).astype(BF16)
        a_sc[CONV_HALO:, :] = _rms(h_ref[...], g).astype(BF16)
        o_ref[...] = h_ref[...]

    a = a_sc[...]

    def conv(u, cw_ref, cb_ref, lo, hi):
        cw = cw_ref[:, lo:hi]
        c = cb_ref[:, lo:hi] + pltpu.roll(u, 2, 0)[CONV_HALO:] * cw[0:1]
        c = c + pltpu.roll(u, 1, 0)[CONV_HALO:] * cw[1:2]
        return c + u[CONV_HALO:] * cw[2:3]

    tf = wo_ref.shape[0]
    slabs = [(lo, lo + FFN_SLAB) for lo in range(0, tf, FFN_SLAB)]
    us = [(_dot(a, wg_ref[:, lo:hi]), _dot(a, wv_ref[:, lo:hi])) for lo, hi in slabs]
    contrib = None
    for (lo, hi), (ug, uv) in zip(slabs, us):
        gate = conv(ug, cwg_ref, cbg_ref, lo, hi)
        val = conv(uv, cwv_ref, cbv_ref, lo, hi)
        act = (gate / (1.0 + jnp.exp(-gate))) * val
        part = _dot(act.astype(BF16), wo_ref[lo:hi, :])
        contrib = part if contrib is None else contrib + part
    o_ref[...] += contrib

    if final:
        @pl.when(j == pl.num_programs(1) - 1)
        def _():
            o_ref[...] = _rms(o_ref[...], gf_ref[...])


def _ffn(h, gain, layer, w_in, conv_w, conv_b, w_out, final_gain, final, tm=1024, tf=512):
    t, dm = h.shape
    ff = w_out.shape[1]
    tm = min(tm, t)
    nf = ff // tf
    halo_blocks = tm // CONV_HALO
    once = pl.Buffered(1)
    return pl.pallas_call(
        functools.partial(_ffn_kernel, tm=tm, final=final),
        out_shape=jax.ShapeDtypeStruct((t, dm), F32),
        grid=(t // tm, nf),
        in_specs=[
            pl.BlockSpec((tm, dm), lambda i, j: (i, 0), pipeline_mode=once),
            pl.BlockSpec((CONV_HALO, dm), lambda i, j: (jnp.maximum(i * halo_blocks - 1, 0), 0)),
            pl.BlockSpec((1, dm), lambda i, j: (0, 0)),
            pl.BlockSpec((None, dm, tf), lambda i, j: (layer, 0, j)),
            pl.BlockSpec((None, dm, tf), lambda i, j: (layer, 0, j + nf)),
            pl.BlockSpec((None, CONV_WIDTH, tf), lambda i, j: (layer, 0, j)),
            pl.BlockSpec((None, CONV_WIDTH, tf), lambda i, j: (layer, 0, j + nf)),
            pl.BlockSpec((None, 1, tf), lambda i, j: (layer, 0, j)),
            pl.BlockSpec((None, 1, tf), lambda i, j: (layer, 0, j + nf)),
            pl.BlockSpec((None, tf, dm), lambda i, j: (layer, j, 0)),
            pl.BlockSpec((1, dm), lambda i, j: (0, 0)),
        ],
        out_specs=pl.BlockSpec((tm, dm), lambda i, j: (i, 0)),
        scratch_shapes=[pltpu.VMEM((tm + CONV_HALO, dm), BF16)],
        compiler_params=_cparams("arbitrary", "arbitrary"),
        name="conv_ffn",
    )(h, h, gain.reshape(1, dm), w_in, w_in, conv_w, conv_w, conv_b[:, None, :], conv_b[:, None, :], w_out,
      final_gain.reshape(1, dm))


def _mla(h, gain, positions, w_in, g_q, g_kv, w_qb, w_kvb, w_o, tm=1024):
    t, dm = h.shape
    tm = min(tm, t)
    hh = MLA_HEADS
    qk = MLA_NOPE + MLA_ROPE
    lat_w = MLA_Q_RANK + MLA_KV_RANK + LANES
    cos_t, sin_t = _rope_tables(positions)
    w_in_p = jnp.pad(w_in, ((0, 0), (0, lat_w - w_in.shape[1]))).astype(BF16)
    w_qb_p = jnp.pad(w_qb.reshape(MLA_Q_RANK, hh, qk), ((0, 0), (0, 0), (0, 2 * LANES - qk)))
    w_qb_p = w_qb_p.reshape(MLA_Q_RANK, hh * 2 * LANES).astype(BF16)

    row128 = pl.BlockSpec((tm, LANES), lambda i, j: (i, 0))

    def lat_epilogue(y, extras, outs):
        gq_ref, gkv_ref, c_ref, s_ref = extras
        cq_ref, ckv_ref, kr_ref = outs
        cq_ref[...] = _rms(y[:, :MLA_Q_RANK], gq_ref[...]).astype(BF16)
        ckv_ref[...] = _rms(y[:, MLA_Q_RANK:MLA_Q_RANK + MLA_KV_RANK], gkv_ref[...]).astype(BF16)
        kr_ref[...] = _rope128(y[:, MLA_Q_RANK + MLA_KV_RANK:], c_ref[...], s_ref[...]).astype(BF16)

    rank_row = pl.BlockSpec((tm, MLA_Q_RANK), lambda i, j: (i, 0))
    c_q, c_kv, k_rope = _proj(
        "mla_latents", h, gain, w_in_p,
        extras=[(g_q.reshape(1, -1), pl.BlockSpec((1, MLA_Q_RANK), lambda i, j: (0, 0))),
                (g_kv.reshape(1, -1), pl.BlockSpec((1, MLA_KV_RANK), lambda i, j: (0, 0))),
                (cos_t, row128), (sin_t, row128)],
        outs=[((t, MLA_Q_RANK), BF16, rank_row), ((t, MLA_KV_RANK), BF16, rank_row), ((t, LANES), BF16, row128)],
        epilogue=lat_epilogue, tm=tm, tn=lat_w)

    heads_per_tile = 4
    tn = heads_per_tile * 2 * LANES
    scale = qk ** -0.5 * LOG2E

    def q_epilogue(y, extras, outs):
        c_ref, s_ref = extras
        (q_ref,) = outs
        y = y * scale
        for a in range(heads_per_tile):
            base = a * 2 * LANES
            q_ref[a, :, :LANES] = y[:, base:base + LANES].astype(BF16)
            q_ref[a, :, LANES:] = _rope128(y[:, base + LANES:base + 2 * LANES], c_ref[...], s_ref[...]).astype(BF16)

    head_blk = pl.BlockSpec((heads_per_tile, tm, 2 * LANES), lambda i, j: (j, i, 0))
    (q_cat,) = _proj(
        "mla_q", c_q, None, w_qb_p,
        extras=[(cos_t, row128), (sin_t, row128)],
        outs=[((hh, t, 2 * LANES), BF16, head_blk)],
        epilogue=q_epilogue, tm=tm, tn=tn)

    def kv_epilogue(y, extras, outs):
        (kr_ref,) = extras
        k_ref, v_ref = outs
        for a in range(heads_per_tile):
            base = a * 2 * LANES
            k_ref[a, :, :LANES] = y[:, base:base + LANES].astype(BF16)
            k_ref[a, :, LANES:] = kr_ref[...]
            v_ref[a, :, :LANES] = y[:, base + LANES:base + 2 * LANES].astype(BF16)
            v_ref[a, :, LANES:] = jnp.ones((tm, LANES), BF16)

    k_cat, v = _proj(
        "mla_kv", c_kv, None, w_kvb.astype(BF16),
        extras=[(k_rope, row128)],
        outs=[((hh, t, 2 * LANES), BF16, head_blk), ((hh, t, 2 * LANES), BF16, head_blk)],
        epilogue=kv_epilogue, tm=tm, tn=tn)

    o = _flash("mla_attn", q_cat, k_cat, v, 0, 0, 0, hh)
    return _out_proj("mla_out", o, w_o.astype(BF16), jnp.zeros((1, dm), F32), h)


def _cols_epilogue(y, extras, outs):
    b_ref, sc_ref = extras
    (o_ref,) = outs
    o_ref[...] = ((y + b_ref[...]) * sc_ref[...]).astype(BF16)


def _swa(h, gain, rel_bias, w_qkv, b_qkv, sinks, w_o, b_o, tm=1024, tn=512):
    t, dm = h.shape
    tm = min(tm, t)
    n = w_qkv.shape[1]
    nq = SWA_Q_HEADS * SWA_HEAD_DIM
    grp = SWA_Q_HEADS // SWA_KV_HEADS
    colscale = jnp.concatenate([jnp.full((nq,), SWA_HEAD_DIM ** -0.5 * LOG2E, F32), jnp.ones((n - nq,), F32)])[None]
    vec = pl.BlockSpec((1, tn), lambda i, j: (0, j))
    (qkv,) = _proj(
        "swa_qkv", h, gain, w_qkv.astype(BF16),
        extras=[(b_qkv.reshape(1, n), vec), (colscale, vec)],
        outs=[((t, n), BF16, pl.BlockSpec((tm, tn), lambda i, j: (i, j)))],
        epilogue=_cols_epilogue, tm=tm, tn=tn)
    tab = _band_table(rel_bias, 1, SWA_WINDOW - 1)
    sink_rep = jnp.repeat(sinks.astype(F32) * LOG2E, SWA_HEAD_DIM)[None]
    k0 = nq // LANES
    v0 = k0 + SWA_KV_HEADS * SWA_HEAD_DIM // LANES
    (o,) = _band("swa_attn", qkv, lambda j: j, lambda j: k0 + j, lambda j: v0 + j, SWA_KV_HEADS // 2, 1, grp,
                 tab, lambda j: j, sink_rep=sink_rep, tile=4 * BLK)
    return _out_proj("swa_out", o, w_o.astype(BF16), b_o.reshape(1, dm), h)


def _dilated(h, gain, rel_bias, w_qkv, w_o, tm=1024, tn=512):
    t, dm = h.shape
    tm = min(tm, t)
    n = w_qkv.shape[1]
    nq = DIL_HEADS * DIL_HEAD_DIM
    colscale = jnp.concatenate([jnp.full((nq,), DIL_HEAD_DIM ** -0.5 * LOG2E, F32), jnp.ones((n - nq,), F32)])[None]
    vec = pl.BlockSpec((1, tn), lambda i, j: (0, j))
    rates = tuple(rate for _, rate in DIL_PAIRS)

    def qkv_epilogue(y, extras, outs):
        (sc_ref,) = extras
        stage = outs[-1]
        y = y * sc_ref[...]
        for k in range(tn // LANES):
            stage[k] = y[:, k * LANES:(k + 1) * LANES]
        for o_ref, rate in zip(outs[:-1], rates):
            if rate == 1:
                o_ref[0] = y.astype(BF16)
            else:
                for c in range(rate):
                    for k in range(tn // LANES):
                        rows = stage[k, pl.ds(c, tm // rate, stride=rate), :]
                        o_ref[c, :, k * LANES:(k + 1) * LANES] = rows.astype(BF16)

    qkvs = _proj(
        "dil_qkv", h, gain, w_qkv.astype(BF16),
        extras=[(colscale, vec)],
        outs=[((rate, t // rate, n), BF16, pl.BlockSpec((rate, tm // rate, tn), lambda i, j: (0, i, j)))
              for rate in rates],
        epilogue=qkv_epilogue, tm=tm, tn=tn, scratch=[pltpu.VMEM((tn // LANES, tm, LANES), F32)])
    npairs = 8
    groups = DIL_HEADS // (2 * npairs)
    os_, ms, ls = [], [], []
    for (window, rate), qkv in zip(DIL_PAIRS, qkvs):
        n_keys = window // rate
        tab = _band_table(rel_bias, rate, n_keys)
        o, m, l = _band(
            f"dil_attn_r{rate}", qkv.reshape(t, n),
            lambda j: j, lambda j: groups + j, lambda j: 2 * groups + j,
            groups, npairs, 1, tab, lambda j: j, seq_len=t // rate)
        os_.append(o)
        ms.append(m)
        ls.append(l)
    expand = jnp.repeat(jnp.eye(LANES, DIL_HEADS, dtype=BF16), DIL_HEAD_DIM, axis=1)
    return _dil_out(os_, ms, ls, rates, expand, w_o.astype(BF16), h)


def _fox(h, gain, w_in, b_f, w_o, tm=1024, tn=512):
    t, dm = h.shape
    tm = min(tm, t)
    hh = FOX_HEADS
    hd = hh * FOX_HEAD_DIM
    heads_per_tile = tn // FOX_HEAD_DIM
    colscale = jnp.concatenate([jnp.full((hd,), FOX_HEAD_DIM ** -0.5 * LOG2E, F32), jnp.ones((hd,), F32)])[None]
    w_all = jnp.pad(w_in, ((0, 0), (0, 3 * hd + LANES - w_in.shape[1]))).astype(BF16)

    def qk_epilogue(y, extras, outs):
        (sc_ref,) = extras
        (o_ref,) = outs
        y = y * sc_ref[...]
        for a in range(heads_per_tile):
            o_ref[a] = y[:, a * FOX_HEAD_DIM:(a + 1) * FOX_HEAD_DIM].astype(BF16)

    (qk,) = _proj(
        "fox_qk", h, gain, w_all,
        extras=[(colscale, pl.BlockSpec((1, tn), lambda i, j: (0, j)))],
        outs=[((2 * hh, t, FOX_HEAD_DIM), BF16,
               pl.BlockSpec((heads_per_tile, tm, FOX_HEAD_DIM), lambda i, j: (j, i, 0)))],
        epilogue=qk_epilogue, tm=tm, tn=tn, col0=0, n=2 * hd)

    def v_epilogue(y, extras, outs):
        (o_ref,) = outs
        for a in range(heads_per_tile):
            o_ref[a, :, :LANES] = y[:, a * FOX_HEAD_DIM:(a + 1) * FOX_HEAD_DIM].astype(BF16)
            o_ref[a, :, LANES:] = jnp.ones((tm, LANES), BF16)

    (v_aug,) = _proj(
        "fox_v", h, gain, w_all,
        extras=[],
        outs=[((hh, t, 2 * LANES), BF16, pl.BlockSpec((heads_per_tile, tm, 2 * LANES), lambda i, j: (j, i, 0)))],
        epilogue=v_epilogue, tm=tm, tn=tn, col0=2 * hd, n=hd)

    def gate_epilogue(y, extras, outs):
        (b_ref,) = extras
        (o_ref,) = outs
        x = y + b_ref[...]
        o_ref[...] = jnp.minimum(x, 0.0) - jnp.log(1.0 + jnp.exp(-jnp.abs(x)))

    b_gate = jnp.pad(b_f.astype(F32), (0, LANES - hh))[None]
    (log_f,) = _proj(
        "fox_gate", h, gain, w_all,
        extras=[(b_gate, pl.BlockSpec((1, LANES), lambda i, j: (0, 0)))],
        outs=[((t, LANES), F32, pl.BlockSpec((tm, LANES), lambda i, j: (i, 0)))],
        epilogue=gate_epilogue, tm=tm, tn=LANES, col0=3 * hd, n=LANES)
    cum, cum_t = _cumsum(log_f)
    o = _flash("fox_attn", qk, qk, v_aug, 0, hh, 0, hh, cum=cum, cumt=cum_t)
    return _out_proj("fox_out", o, w_o.astype(BF16), jnp.zeros((1, dm), F32), h)


def kernel(x, positions, rel_bias, norm_mix, norm_ffn, mla_w_in, mla_g_q, mla_g_kv, mla_w_qb, mla_w_kvb, mla_w_o, swa_w_qkv, swa_b_qkv, swa_sinks, swa_w_o, swa_b_o, dil_w_qkv, dil_w_o, fox_w_in, fox_b_f, fox_w_o, ffn_w_in, ffn_conv_w, ffn_conv_b, ffn_w_out, final_norm):
    bsz, t, dm = x.shape
    assert bsz == 1
    depth = norm_mix.shape[0]
    h = x.reshape(t, dm)
    pos = positions.reshape(t)
    ffn_w_in_b = ffn_w_in.astype(BF16)
    ffn_w_out_b = ffn_w_out.astype(BF16)
    for i in range(depth):
        kind = i % 4
        r = i // 4
        if kind == 0:
            h = _mla(h, norm_mix[i], pos, mla_w_in[r], mla_g_q[r], mla_g_kv[r], mla_w_qb[r], mla_w_kvb[r], mla_w_o[r])
        elif kind == 1:
            h = _swa(h, norm_mix[i], rel_bias, swa_w_qkv[r], swa_b_qkv[r], swa_sinks[r], swa_w_o[r], swa_b_o[r])
        elif kind == 2:
            h = _dilated(h, norm_mix[i], rel_bias, dil_w_qkv[r], dil_w_o[r])
        else:
            h = _fox(h, norm_mix[i], fox_w_in[r], fox_b_f[r], fox_w_o[r])
        h = _ffn(h, norm_ffn[i], i, ffn_w_in_b, ffn_conv_w, ffn_conv_b, ffn_w_out_b, final_norm,
                 final=(i == depth - 1))
    return h.reshape(bsz, t, dm)
```

```python
import functools
import math

import jax
import jax.numpy as jnp
from jax import lax
from jax.experimental import pallas as pl
from jax.experimental.pallas import tpu as pltpu

F32 = jnp.float32
BF16 = jnp.bfloat16
EPS = 1e-6
NEG = -1e30
LOG2E = math.log2(math.e)

LANES = 128
VMEM_LIMIT_BYTES = 56 * 1024 * 1024
BLK = 128

ROPE_THETA = 10000.0
REL_BUCKETS = 32
REL_MAX_DIST = 2048
MLA_HEADS, MLA_Q_RANK, MLA_KV_RANK, MLA_NOPE, MLA_ROPE, MLA_V = 16, 512, 512, 128, 64, 128
SWA_Q_HEADS, SWA_KV_HEADS, SWA_HEAD_DIM, SWA_WINDOW = 32, 4, 64, 128
DIL_HEADS, DIL_HEAD_DIM = 32, 64
DIL_PAIRS = ((128, 1), (512, 4), (2048, 16))
FOX_HEADS, FOX_HEAD_DIM = 16, 128
CONV_WIDTH = 3
CONV_HALO = 16
FFN_SLAB = 512
FLASH_HEADS = 1


def _cparams(*sem):
    return pltpu.CompilerParams(dimension_semantics=sem, vmem_limit_bytes=VMEM_LIMIT_BYTES)


def _rms(x, g):
    return x * lax.rsqrt(jnp.mean(x * x, axis=-1, keepdims=True) + EPS) * g


def _dot(a, b):
    return jnp.dot(a, b, preferred_element_type=F32)


def _dot_nt(a, b):
    return lax.dot_general(a, b, (((1,), (1,)), ((), ())), preferred_element_type=F32)


def _rope_tab_kernel(pos_ref, f_ref, sgn_ref, c_ref, s_ref):
    ang = pos_ref[...].astype(F32) * f_ref[...]
    c_ref[...] = jnp.cos(ang)
    s_ref[...] = jnp.sin(ang) * sgn_ref[...]


def _rope_tables(positions):
    t = positions.shape[0]
    half = MLA_ROPE // 2
    inv_freq = ROPE_THETA ** (-jnp.arange(half, dtype=F32) / half)
    zeros = jnp.zeros((LANES - 2 * half,), F32)
    freq = jnp.concatenate([inv_freq, inv_freq, zeros])[None]
    sgn = jnp.concatenate([-jnp.ones((half,), F32), jnp.ones((half,), F32), zeros])[None]
    tm = min(t, 1024)
    row = pl.BlockSpec((tm, LANES), lambda i: (i, 0))
    vec = pl.BlockSpec((1, LANES), lambda i: (0, 0))
    return pl.pallas_call(
        _rope_tab_kernel,
        out_shape=(jax.ShapeDtypeStruct((t, LANES), F32),) * 2,
        grid=(t // tm,),
        in_specs=[pl.BlockSpec((tm, 1), lambda i: (i, 0)), vec, vec],
        out_specs=(row, row),
        compiler_params=_cparams("arbitrary"),
        name="rope_tables",
    )(positions.reshape(t, 1), freq, sgn)


def _rope128(z, c, s):
    half = MLA_ROPE // 2
    lane = lax.broadcasted_iota(jnp.int32, z.shape, 1)
    swapped = jnp.where(lane < half, pltpu.roll(z, LANES - half, 1), pltpu.roll(z, half, 1))
    return z * c + swapped * s


def _proj_kernel(*refs, norm, n_extra, epilogue):
    if norm:
        lhs_ref, g_ref, w_ref = refs[:3]
        rest = refs[3:]
        a_ref = rest[-1]
        rest = rest[:-1]

        @pl.when(pl.program_id(1) == 0)
        def _():
            a_ref[...] = _rms(lhs_ref[...], g_ref[...]).astype(BF16)

        a = a_ref[...]
    else:
        lhs_ref, w_ref = refs[:2]
        rest = refs[2:]
        a = lhs_ref[...]
    y = _dot(a, w_ref[...])
    epilogue(y, rest[:n_extra], rest[n_extra:])


def _proj(name, lhs, gain, w, extras, outs, epilogue, tm, tn, col0=0, n=None, scratch=()):
    m, k = lhs.shape
    n = w.shape[1] if n is None else n
    woff = col0 // tn
    norm = gain is not None
    in_specs = [pl.BlockSpec((tm, k), lambda i, j: (i, 0))]
    args = [lhs]
    if norm:
        in_specs.append(pl.BlockSpec((1, k), lambda i, j: (0, 0)))
        args.append(gain.reshape(1, k))
    in_specs.append(pl.BlockSpec((k, tn), lambda i, j: (0, woff + j)))
    args.append(w)
    for arr, spec in extras:
        in_specs.append(spec)
        args.append(arr)
    res = pl.pallas_call(
        functools.partial(_proj_kernel, norm=norm, n_extra=len(extras), epilogue=epilogue),
        out_shape=tuple(jax.ShapeDtypeStruct(s, d) for s, d, _ in outs),
        grid=(m // tm, n // tn),
        in_specs=in_specs,
        out_specs=tuple(spec for _, _, spec in outs),
        scratch_shapes=list(scratch) + ([pltpu.VMEM((tm, k), BF16)] if norm else []),
        compiler_params=_cparams("arbitrary", "arbitrary"),
        name=name,
    )(*args)
    return res


def _out_proj_kernel(lhs_ref, w_ref, b_ref, res_ref, o_ref):
    o_ref[...] = res_ref[...] + (_dot(lhs_ref[...], w_ref[...]) + b_ref[...])


def _out_proj(name, lhs, w, bias, res, tm=512):
    m, k = lhs.shape
    n = w.shape[1]
    tm = min(tm, m)
    return pl.pallas_call(
        _out_proj_kernel,
        out_shape=jax.ShapeDtypeStruct((m, n), F32),
        grid=(m // tm,),
        in_specs=[
            pl.BlockSpec((tm, k), lambda i: (i, 0)),
            pl.BlockSpec((k, n), lambda i: (0, 0)),
            pl.BlockSpec((1, n), lambda i: (0, 0)),
            pl.BlockSpec((tm, n), lambda i: (i, 0)),
        ],
        out_specs=pl.BlockSpec((tm, n), lambda i: (i, 0)),
        compiler_params=_cparams("arbitrary"),
        name=name,
    )(lhs, w, bias, res)


def _flash_kernel(*refs, bq, fox):
    if fox:
        q_ref, k_ref, v_ref, cum_ref, cumt_ref, o_ref, s_sc, m_sc, acc_sc = refs
    else:
        q_ref, k_ref, v_ref, o_ref, s_sc, m_sc, acc_sc = refs
    nh = q_ref.shape[0]
    g = pl.program_id(0)
    i = pl.program_id(1)
    reps = bq // LANES
    m_sc[...] = jnp.full(m_sc.shape, NEG, F32)
    acc_sc[...] = jnp.zeros(acc_sc.shape, F32)
    q = [q_ref[a] for a in range(nh)]
    if fox:
        lane = lax.broadcasted_iota(jnp.int32, (bq, LANES), 1)
        cq = [jnp.broadcast_to(jnp.sum(jnp.where(lane == nh * g + a, cum_ref[...], 0.0), axis=1, keepdims=True),
                               (bq, LANES)) for a in range(nh)]

    def scores(c, slot):
        k0 = pl.multiple_of(c * bq, bq)
        for a in range(nh):
            s_sc[2 * a + slot] = _dot_nt(q[a], k_ref[a, pl.ds(k0, bq), :])

    def softmax_pv(c, slot, masked):
        k0 = pl.multiple_of(c * bq, bq)
        for a in range(nh):
            t = s_sc[2 * a + slot]
            if fox:
                t = t - cumt_ref[pl.ds(nh * g + a, 1), pl.ds(k0, bq)]
            if masked:
                row = lax.broadcasted_iota(jnp.int32, (bq, bq), 0)
                col = lax.broadcasted_iota(jnp.int32, (bq, bq), 1)
                t = jnp.where(col <= row, t, NEG)
            r = jnp.max(t, axis=1, keepdims=True)
            m_prev = m_sc[a]
            m_new = jnp.maximum(m_prev, r + cq[a]) if fox else jnp.maximum(m_prev, r)
            alpha = jnp.exp2(m_prev - m_new)
            shift = m_new - cq[a] if fox else m_new
            p = jnp.exp2(t - jnp.tile(shift, (1, reps)))
            pv = _dot(p.astype(BF16), v_ref[a, pl.ds(k0, bq), :])
            acc_sc[a] = jnp.tile(alpha, (1, 2)) * acc_sc[a] + pv
            m_sc[a] = m_new

    scores(0, 0)

    def pair(c):
        scores(c + 1, 1)
        softmax_pv(c, 0, False)
        scores(c + 2, 0)
        softmax_pv(c + 1, 1, False)

    def body(u, carry):
        for a in range(4):
            pair(8 * u + 2 * a)
        return carry

    n_octs = lax.shift_right_logical(i, 3)
    lax.fori_loop(0, n_octs, body, 0)
    done = 8 * n_octs

    @pl.when(lax.bitwise_and(i, 4) == 4)
    def _():
        pair(done)
        pair(done + 2)

    done = done + lax.bitwise_and(i, 4)

    @pl.when(lax.bitwise_and(i, 2) == 2)
    def _():
        pair(done)

    @pl.when(lax.bitwise_and(i, 1) == 1)
    def _():
        scores(i, 1)
        softmax_pv(i - 1, 0, False)
        softmax_pv(i, 1, True)

    @pl.when(lax.bitwise_and(i, 1) == 0)
    def _():
        softmax_pv(i, 0, True)

    for a in range(nh):
        acc = acc_sc[a]
        o_ref[:, a * LANES:(a + 1) * LANES] = (acc[:, :LANES] / acc[:, LANES:]).astype(BF16)


def _flash(name, q, k, v, q_off, k_off, v_off, heads, cum=None, cumt=None, bq=512, nh=FLASH_HEADS):
    t = q.shape[1]
    dk = q.shape[2]
    dv = v.shape[2]
    assert dv == 2 * LANES
    assert heads % nh == 0 and q_off % nh == 0 and k_off % nh == 0 and v_off % nh == 0
    bq = min(bq, t)
    fox = cum is not None
    in_specs = [
        pl.BlockSpec((nh, bq, dk), lambda h, i: (q_off // nh + h, i, 0)),
        pl.BlockSpec((nh, t, dk), lambda h, i: (k_off // nh + h, 0, 0)),
        pl.BlockSpec((nh, t, dv), lambda h, i: (v_off // nh + h, 0, 0)),
    ]
    args = [q, k, v]
    if fox:
        in_specs += [
            pl.BlockSpec((bq, LANES), lambda h, i: (i, 0)),
            pl.BlockSpec(cumt.shape, lambda h, i: (0, 0)),
        ]
        args += [cum, cumt]
    return pl.pallas_call(
        functools.partial(_flash_kernel, bq=bq, fox=fox),
        out_shape=jax.ShapeDtypeStruct((t, heads * LANES), BF16),
        grid=(heads // nh, t // bq),
        in_specs=in_specs,
        out_specs=pl.BlockSpec((bq, nh * LANES), lambda h, i: (i, h)),
        scratch_shapes=[pltpu.VMEM((2 * nh, bq, bq), F32), pltpu.VMEM((nh, bq, LANES), F32),
                        pltpu.VMEM((nh, bq, dv), F32)],
        compiler_params=_cparams("arbitrary", "arbitrary"),
        name=name,
    )(*args)


def _band_kernel(*refs, nbk, npairs, grp, sinks, tiles_per_seq):
    if sinks:
        q_ref, k_ref, v_ref, kp_ref, vp_ref, tab_ref, sink_ref, o_ref, kf_sc, vf_sc = refs
    else:
        q_ref, k_ref, v_ref, kp_ref, vp_ref, tab_ref, o_ref, m_ref, l_ref, kf_sc, vf_sc = refs
    i = pl.program_id(0)
    j = pl.program_id(1)
    kf_sc[0:BLK, :] = kp_ref[...]
    kf_sc[BLK:, :] = k_ref[...]
    vf_sc[0:BLK, :] = vp_ref[...]
    vf_sc[BLK:, :] = v_ref[...]
    d = LANES // 2
    n_kv = 2 * npairs
    col = lax.broadcasted_iota(jnp.int32, (1, 2 * BLK), 1)
    penrow = jnp.where(jnp.logical_and(col < BLK, i % tiles_per_seq == 0), NEG, 0.0)
    if not sinks:
        stat_lane = lax.broadcasted_iota(jnp.int32, (BLK, LANES), 1)
        steps_per_stat = LANES // n_kv

        @pl.when(j % steps_per_stat == 0)
        def _():
            m_ref[...] = jnp.zeros(m_ref.shape, F32)
            l_ref[...] = jnp.zeros(l_ref.shape, F32)

    lo_k = lax.broadcasted_iota(jnp.int32, (2 * BLK, LANES), 1) < d
    lo_o = lax.broadcasted_iota(jnp.int32, (BLK, LANES), 1) < d
    swap = lambda a: jnp.concatenate([a[:, d:], a[:, :d]], axis=1)

    for b in range(nbk):
        r0 = b * BLK
        if not sinks:
            m_st = m_ref[r0:r0 + BLK, :]
            l_st = l_ref[r0:r0 + BLK, :]
        for kp in range(npairs):
            kk = kf_sc[r0:r0 + 2 * BLK, kp * LANES:(kp + 1) * LANES]
            vv = vf_sc[r0:r0 + 2 * BLK, kp * LANES:(kp + 1) * LANES]
            zero = jnp.zeros_like(kk)
            k_own = [jnp.where(lo_k, kk, zero), jnp.where(lo_k, zero, kk)]
            if grp > 1:
                k_by = [[k_own[0], swap(k_own[0])], [swap(k_own[1]), k_own[1]]]
                v_by = [[vv, swap(vv)], [swap(vv), vv]]
            else:
                k_by = [[k_own[0], None], [None, k_own[1]]]
                v_by = [[vv, None], [None, vv]]
            for c in range(grp):
                qcol = kp * grp + c
                qq = q_ref[r0:r0 + BLK, qcol * LANES:(qcol + 1) * LANES]
                halves = []
                for qh in range(2):
                    hq = 2 * qcol + qh
                    kh = hq // grp - 2 * kp
                    s = _dot_nt(qq, k_by[kh][qh]) + tab_ref[hq]
                    if b == 0:
                        s = s + penrow
                    m = jnp.max(s, axis=1, keepdims=True)
                    p = jnp.exp2(s - m)
                    l = jnp.sum(p, axis=1, keepdims=True)
                    acc = _dot(p.astype(BF16), v_by[kh][qh])
                    if sinks:
                        sink = sink_ref[:, qcol * LANES:(qcol + 1) * LANES]
                        m2 = jnp.maximum(m, sink)
                        a = jnp.exp2(m - m2)
                        den = l * a + jnp.exp2(sink - m2)
                        halves.append(acc * (a / den))
                    else:
                        halves.append(acc / l)
                        sel = stat_lane == (n_kv * j + hq) % LANES
                        m_st = jnp.where(sel, m, m_st)
                        l_st = jnp.where(sel, l, l_st)
                o_ref[r0:r0 + BLK, qcol * LANES:(qcol + 1) * LANES] = jnp.where(lo_o, halves[0], halves[1]).astype(BF16)
        if not sinks:
            m_ref[r0:r0 + BLK, :] = m_st
            l_ref[r0:r0 + BLK, :] = l_st


def _band(name, x, q_blk, k_blk, v_blk, n_steps, npairs, grp, tab, tab_blk, sink_rep=None, tile=512, seq_len=None):
    length = x.shape[0]
    seq_len = length if seq_len is None else seq_len
    tb = min(tile, seq_len)
    nbk = tb // BLK
    kw = LANES * npairs
    qw = kw * grp
    n_kv = 2 * npairs
    n_out = n_steps * qw
    sinks = sink_rep is not None
    prev = lambda i, j: jnp.maximum(i * nbk - 1, 0)
    in_specs = [
        pl.BlockSpec((tb, qw), lambda i, j: (i, q_blk(j))),
        pl.BlockSpec((tb, kw), lambda i, j: (i, k_blk(j))),
        pl.BlockSpec((tb, kw), lambda i, j: (i, v_blk(j))),
        pl.BlockSpec((BLK, kw), lambda i, j: (prev(i, j), k_blk(j))),
        pl.BlockSpec((BLK, kw), lambda i, j: (prev(i, j), v_blk(j))),
        pl.BlockSpec((n_kv * grp, BLK, 2 * BLK), lambda i, j: (tab_blk(j), 0, 0)),
    ]
    args = [x, x, x, x, x, tab]
    out_shape = [jax.ShapeDtypeStruct((length, n_out), BF16)]
    out_specs = [pl.BlockSpec((tb, qw), lambda i, j: (i, j))]
    if sinks:
        in_specs.append(pl.BlockSpec((1, qw), lambda i, j: (0, j)))
        args.append(sink_rep)
    else:
        n_stat = max(LANES, n_kv * n_steps)
        stat = pl.BlockSpec((tb, LANES), lambda i, j: (i, j // (LANES // n_kv)))
        out_shape += [jax.ShapeDtypeStruct((length, n_stat), F32)] * 2
        out_specs += [stat, stat]
    return pl.pallas_call(
        functools.partial(_band_kernel, nbk=nbk, npairs=npairs, grp=grp, sinks=sinks, tiles_per_seq=seq_len // tb),
        out_shape=tuple(out_shape),
        grid=(length // tb, n_steps),
        in_specs=in_specs,
        out_specs=tuple(out_specs),
        scratch_shapes=[pltpu.VMEM((tb + BLK, kw), BF16), pltpu.VMEM((tb + BLK, kw), BF16)],
        compiler_params=_cparams("arbitrary", "arbitrary"),
        name=name,
    )(*args)


def _t5_bucket(n):
    exact = REL_BUCKETS // 2
    nf = jnp.maximum(n, 1).astype(F32)
    large = exact + (jnp.log(nf / exact) / math.log(REL_MAX_DIST / exact) * (REL_BUCKETS - exact)).astype(jnp.int32)
    return jnp.where(n < exact, n, jnp.minimum(large, REL_BUCKETS - 1))


def _band_table(rel_bias, rate, max_dist):
    nh = rel_bias.shape[1]
    w = 2 * BLK
    dist = jnp.arange(w)
    row = rel_bias[_t5_bucket(rate * jnp.minimum(dist, max_dist))].T * LOG2E
    f = jnp.where(dist[None] <= max_dist, row, NEG).astype(F32)
    r = jnp.roll(f[:, ::-1], BLK + 1, axis=1)
    rr = jnp.concatenate([r, r], axis=1)
    flat = jnp.broadcast_to(rr[:, None, :], (nh, BLK + 1, 2 * w)).reshape(nh, (BLK + 1) * 2 * w)
    return flat[:, w:w + BLK * (2 * w - 1)].reshape(nh, BLK, 2 * w - 1)[:, :, :w]


def _dil_out_kernel(o1, o2, o3, m1, m2, m3, l1, l2, l3, e_ref, w_ref, res_ref, out_ref, row_sc, stat_sc, *, tm, rates):
    def in_order(ref, rate, sc):
        if rate == 1:
            return ref[0].astype(F32)
        groups = ref.shape[2] // LANES
        for c in range(rate):
            rows = ref[c].astype(F32)
            for k in range(groups):
                sc[k, pl.ds(c, tm // rate, stride=rate), :] = rows[:, k * LANES:(k + 1) * LANES]
        return jnp.concatenate([sc[k] for k in range(groups)], axis=1)

    ms = [in_order(m, r, stat_sc) for m, r in zip((m1, m2, m3), rates)]
    ls = [in_order(l, r, stat_sc) for l, r in zip((l1, l2, l3), rates)]
    lane = lax.broadcasted_iota(jnp.int32, ms[0].shape, 1)
    mx = jnp.maximum(jnp.maximum(ms[0], ms[1]), ms[2])
    ws = [l * jnp.exp2(m - mx) for m, l in zip(ms, ls)]
    den = ws[0] + ws[1] + ws[2]
    e = e_ref[...]
    o = None
    for w, o_ref, r in zip(ws, (o1, o2, o3), rates):
        wn = jnp.where(lane < DIL_HEADS, w / den, 0.0)
        hi = wn.astype(BF16)
        lo = (wn - hi.astype(F32)).astype(BF16)
        term = in_order(o_ref, r, row_sc) * (_dot(hi, e) + _dot(lo, e))
        o = term if o is None else o + term
    out_ref[...] = res_ref[...] + _dot(o.astype(BF16), w_ref[...])


def _dil_out(os_, ms, ls, rates, expand, w, res, tm=512):
    t, n = res.shape
    tm = min(tm, t)
    row = pl.BlockSpec((tm, n), lambda i: (i, 0))

    def by_residue(a, rate):
        width = a.shape[1]
        return a.reshape(rate, t // rate, width), pl.BlockSpec((rate, tm // rate, width), lambda i: (0, i, 0))

    branch = [by_residue(a, r) for group in (os_, ms, ls) for a, r in zip(group, rates)]
    return pl.pallas_call(
        functools.partial(_dil_out_kernel, tm=tm, rates=tuple(rates)),
        out_shape=jax.ShapeDtypeStruct((t, n), F32),
        grid=(t // tm,),
        in_specs=[spec for _, spec in branch] + [
            pl.BlockSpec((LANES, n), lambda i: (0, 0)),
            pl.BlockSpec((n, n), lambda i: (0, 0)),
            row,
        ],
        out_specs=row,
        scratch_shapes=[pltpu.VMEM((n // LANES, tm, LANES), F32), pltpu.VMEM((1, tm, LANES), F32)],
        compiler_params=_cparams("arbitrary"),
        name="dil_out",
    )(*[a for a, _ in branch], expand, w, res)


def _cumsum_kernel(x_ref, c_ref, ct_ref, carry_sc, *, tb):
    @pl.when(pl.program_id(0) == 0)
    def _():
        carry_sc[...] = jnp.zeros(carry_sc.shape, F32)

    x = x_ref[...]
    row = lax.broadcasted_iota(jnp.int32, (tb, tb), 0)
    colm = lax.broadcasted_iota(jnp.int32, (tb, tb), 1)
    tri = jnp.where(colm <= row, 1.0, 0.0).astype(BF16)
    hi = x.astype(BF16)
    r1 = x - hi.astype(F32)
    mid = r1.astype(BF16)
    lo = (r1 - mid.astype(F32)).astype(BF16)
    c = carry_sc[...] + (_dot(tri, hi) + _dot(tri, mid) + _dot(tri, lo))
    c2 = c * LOG2E
    c_ref[...] = c2
    ct_ref[...] = c2.T
    carry_sc[...] = c[tb - 1:tb, :]


def _cumsum(x, tb=256):
    t = x.shape[0]
    tb = min(tb, t)
    return pl.pallas_call(
        functools.partial(_cumsum_kernel, tb=tb),
        out_shape=(jax.ShapeDtypeStruct((t, LANES), F32), jax.ShapeDtypeStruct((LANES, t), F32)),
        grid=(t // tb,),
        in_specs=[pl.BlockSpec((tb, LANES), lambda i: (i, 0))],
        out_specs=(pl.BlockSpec((tb, LANES), lambda i: (i, 0)), pl.BlockSpec((LANES, tb), lambda i: (0, i))),
        scratch_shapes=[pltpu.VMEM((1, LANES), F32)],
        compiler_params=_cparams("arbitrary"),
        name="fox_cumsum",
    )(x)


def _ffn_kernel(h_ref, hp_ref, g_ref, wg_ref, wv_ref, cwg_ref, cwv_ref, cbg_ref, cbv_ref, wo_ref, gf_ref, o_ref, a_sc,
                *, tm, final):
    i = pl.program_id(0)
    j = pl.program_id(1)

    @pl.when(j == 0)
    def _():
        g = g_ref[...]
        halo = _rms(hp_ref[...], g)
        a_sc[0:CONV_HALO, :] = jnp.where(i == 0, 0.0, halo).astype(BF16)
        a_sc[CONV_HALO:, :] = _rms(h_ref[...], g).astype(BF16)
        o_ref[...] = h_ref[...]

    a = a_sc[...]

    def conv(u, cw_ref, cb_ref, lo, hi):
        cw = cw_ref[:, lo:hi]
        c = cb_ref[:, lo:hi] + pltpu.roll(u, 2, 0)[CONV_HALO:] * cw[0:1]
        c = c + pltpu.roll(u, 1, 0)[CONV_HALO:] * cw[1:2]
        return c + u[CONV_HALO:] * cw[2:3]

    tf = wo_ref.shape[0]
    slabs = [(lo, lo + FFN_SLAB) for lo in range(0, tf, FFN_SLAB)]
    us = [(_dot(a, wg_ref[:, lo:hi]), _dot(a, wv_ref[:, lo:hi])) for lo, hi in slabs]
    contrib = None
    for (lo, hi), (ug, uv) in zip(slabs, us):
        gate = conv(ug, cwg_ref, cbg_ref, lo, hi)
        val = conv(uv, cwv_ref, cbv_ref, lo, hi)
        act = (gate / (1.0 + jnp.exp(-gate))) * val
        part = _dot(act.astype(BF16), wo_ref[lo:hi, :])
        contrib = part if contrib is None else contrib + part
    o_ref[...] += contrib

    if final:
        @pl.when(j == pl.num_programs(1) - 1)
        def _():
            o_ref[...] = _rms(o_ref[...], gf_ref[...])


def _ffn(h, gain, layer, w_in, conv_w, conv_b, w_out, final_gain, final, tm=1024, tf=512):
    t, dm = h.shape
    ff = w_out.shape[1]
    tm = min(tm, t)
    nf = ff // tf
    halo_blocks = tm // CONV_HALO
    once = pl.Buffered(1)
    return pl.pallas_call(
        functools.partial(_ffn_kernel, tm=tm, final=final),
        out_shape=jax.ShapeDtypeStruct((t, dm), F32),
        grid=(t // tm, nf),
        in_specs=[
            pl.BlockSpec((tm, dm), lambda i, j: (i, 0), pipeline_mode=once),
            pl.BlockSpec((CONV_HALO, dm), lambda i, j: (jnp.maximum(i * halo_blocks - 1, 0), 0)),
            pl.BlockSpec((1, dm), lambda i, j: (0, 0)),
            pl.BlockSpec((None, dm, tf), lambda i, j: (layer, 0, j)),
            pl.BlockSpec((None, dm, tf), lambda i, j: (layer, 0, j + nf)),
            pl.BlockSpec((None, CONV_WIDTH, tf), lambda i, j: (layer, 0, j)),
            pl.BlockSpec((None, CONV_WIDTH, tf), lambda i, j: (layer, 0, j + nf)),
            pl.BlockSpec((None, 1, tf), lambda i, j: (layer, 0, j)),
            pl.BlockSpec((None, 1, tf), lambda i, j: (layer, 0, j + nf)),
            pl.BlockSpec((None, tf, dm), lambda i, j: (layer, j, 0)),
            pl.BlockSpec((1, dm), lambda i, j: (0, 0)),
        ],
        out_specs=pl.BlockSpec((tm, dm), lambda i, j: (i, 0)),
        scratch_shapes=[pltpu.VMEM((tm + CONV_HALO, dm), BF16)],
        compiler_params=_cparams("arbitrary", "arbitrary"),
        name="conv_ffn",
    )(h, h, gain.reshape(1, dm), w_in, w_in, conv_w, conv_w, conv_b[:, None, :], conv_b[:, None, :], w_out,
      final_gain.reshape(1, dm))


def _mla(h, gain, positions, w_in, g_q, g_kv, w_qb, w_kvb, w_o, tm=1024):
    t, dm = h.shape
    tm = min(tm, t)
    hh = MLA_HEADS
    qk = MLA_NOPE + MLA_ROPE
    lat_w = MLA_Q_RANK + MLA_KV_RANK + LANES
    cos_t, sin_t = _rope_tables(positions)
    w_in_p = jnp.pad(w_in, ((0, 0), (0, lat_w - w_in.shape[1]))).astype(BF16)
    w_qb_p = jnp.pad(w_qb.reshape(MLA_Q_RANK, hh, qk), ((0, 0), (0, 0), (0, 2 * LANES - qk)))
    w_qb_p = w_qb_p.reshape(MLA_Q_RANK, hh * 2 * LANES).astype(BF16)

    row128 = pl.BlockSpec((tm, LANES), lambda i, j: (i, 0))

    def lat_epilogue(y, extras, outs):
        gq_ref, gkv_ref, c_ref, s_ref = extras
        cq_ref, ckv_ref, kr_ref = outs
        cq_ref[...] = _rms(y[:, :MLA_Q_RANK], gq_ref[...]).astype(BF16)
        ckv_ref[...] = _rms(y[:, MLA_Q_RANK:MLA_Q_RANK + MLA_KV_RANK], gkv_ref[...]).astype(BF16)
        kr_ref[...] = _rope128(y[:, MLA_Q_RANK + MLA_KV_RANK:], c_ref[...], s_ref[...]).astype(BF16)

    rank_row = pl.BlockSpec((tm, MLA_Q_RANK), lambda i, j: (i, 0))
    c_q, c_kv, k_rope = _proj(
        "mla_latents", h, gain, w_in_p,
        extras=[(g_q.reshape(1, -1), pl.BlockSpec((1, MLA_Q_RANK), lambda i, j: (0, 0))),
                (g_kv.reshape(1, -1), pl.BlockSpec((1, MLA_KV_RANK), lambda i, j: (0, 0))),
                (cos_t, row128), (sin_t, row128)],
        outs=[((t, MLA_Q_RANK), BF16, rank_row), ((t, MLA_KV_RANK), BF16, rank_row), ((t, LANES), BF16, row128)],
        epilogue=lat_epilogue, tm=tm, tn=lat_w)

    heads_per_tile = 4
    tn = heads_per_tile * 2 * LANES
    scale = qk ** -0.5 * LOG2E

    def q_epilogue(y, extras, outs):
        c_ref, s_ref = extras
        (q_ref,) = outs
        y = y * scale
        for a in range(heads_per_tile):
            base = a * 2 * LANES
            q_ref[a, :, :LANES] = y[:, base:base + LANES].astype(BF16)
            q_ref[a, :, LANES:] = _rope128(y[:, base + LANES:base + 2 * LANES], c_ref[...], s_ref[...]).astype(BF16)

    head_blk = pl.BlockSpec((heads_per_tile, tm, 2 * LANES), lambda i, j: (j, i, 0))
    (q_cat,) = _proj(
        "mla_q", c_q, None, w_qb_p,
        extras=[(cos_t, row128), (sin_t, row128)],
        outs=[((hh, t, 2 * LANES), BF16, head_blk)],
        epilogue=q_epilogue, tm=tm, tn=tn)

    def kv_epilogue(y, extras, outs):
        (kr_ref,) = extras
        k_ref, v_ref = outs
        for a in range(heads_per_tile):
            base = a * 2 * LANES
            k_ref[a, :, :LANES] = y[:, base:base + LANES].astype(BF16)
            k_ref[a, :, LANES:] = kr_ref[...]
            v_ref[a, :, :LANES] = y[:, base + LANES:base + 2 * LANES].astype(BF16)
            v_ref[a, :, LANES:] = jnp.ones((tm, LANES), BF16)

    k_cat, v = _proj(
        "mla_kv", c_kv, None, w_kvb.astype(BF16),
        extras=[(k_rope, row128)],
        outs=[((hh, t, 2 * LANES), BF16, head_blk), ((hh, t, 2 * LANES), BF16, head_blk)],
        epilogue=kv_epilogue, tm=tm, tn=tn)

    o = _flash("mla_attn", q_cat, k_cat, v, 0, 0, 0, hh)
    return _out_proj("mla_out", o, w_o.astype(BF16), jnp.zeros((1, dm), F32), h)


def _cols_epilogue(y, extras, outs):
    b_ref, sc_ref = extras
    (o_ref,) = outs
    o_ref[...] = ((y + b_ref[...]) * sc_ref[...]).astype(BF16)


def _swa(h, gain, rel_bias, w_qkv, b_qkv, sinks, w_o, b_o, tm=1024, tn=512):
    t, dm = h.shape
    tm = min(tm, t)
    n = w_qkv.shape[1]
    nq = SWA_Q_HEADS * SWA_HEAD_DIM
    grp = SWA_Q_HEADS // SWA_KV_HEADS
    colscale = jnp.concatenate([jnp.full((nq,), SWA_HEAD_DIM ** -0.5 * LOG2E, F32), jnp.ones((n - nq,), F32)])[None]
    vec = pl.BlockSpec((1, tn), lambda i, j: (0, j))
    (qkv,) = _proj(
        "swa_qkv", h, gain, w_qkv.astype(BF16),
        extras=[(b_qkv.reshape(1, n), vec), (colscale, vec)],
        outs=[((t, n), BF16, pl.BlockSpec((tm, tn), lambda i, j: (i, j)))],
        epilogue=_cols_epilogue, tm=tm, tn=tn)
    tab = _band_table(rel_bias, 1, SWA_WINDOW - 1)
    sink_rep = jnp.repeat(sinks.astype(F32) * LOG2E, SWA_HEAD_DIM)[None]
    k0 = nq // LANES
    v0 = k0 + SWA_KV_HEADS * SWA_HEAD_DIM // LANES
    (o,) = _band("swa_attn", qkv, lambda j: j, lambda j: k0 + j, lambda j: v0 + j, SWA_KV_HEADS // 2, 1, grp,
                 tab, lambda j: j, sink_rep=sink_rep, tile=4 * BLK)
    return _out_proj("swa_out", o, w_o.astype(BF16), b_o.reshape(1, dm), h)


def _dilated(h, gain, rel_bias, w_qkv, w_o, tm=1024, tn=512):
    t, dm = h.shape
    tm = min(tm, t)
    n = w_qkv.shape[1]
    nq = DIL_HEADS * DIL_HEAD_DIM
    colscale = jnp.concatenate([jnp.full((nq,), DIL_HEAD_DIM ** -0.5 * LOG2E, F32), jnp.ones((n - nq,), F32)])[None]
    vec = pl.BlockSpec((1, tn), lambda i, j: (0, j))
    rates = tuple(rate for _, rate in DIL_PAIRS)

    def qkv_epilogue(y, extras, outs):
        (sc_ref,) = extras
        stage = outs[-1]
        y = y * sc_ref[...]
        for k in range(tn // LANES):
            stage[k] = y[:, k * LANES:(k + 1) * LANES]
        for o_ref, rate in zip(outs[:-1], rates):
            if rate == 1:
                o_ref[0] = y.astype(BF16)
            else:
                for c in range(rate):
                    for k in range(tn // LANES):
                        rows = stage[k, pl.ds(c, tm // rate, stride=rate), :]
                        o_ref[c, :, k * LANES:(k + 1) * LANES] = rows.astype(BF16)

    qkvs = _proj(
        "dil_qkv", h, gain, w_qkv.astype(BF16),
        extras=[(colscale, vec)],
        outs=[((rate, t // rate, n), BF16, pl.BlockSpec((rate, tm // rate, tn), lambda i, j: (0, i, j)))
              for rate in rates],
        epilogue=qkv_epilogue, tm=tm, tn=tn, scratch=[pltpu.VMEM((tn // LANES, tm, LANES), F32)])
    npairs = 8
    groups = DIL_HEADS // (2 * npairs)
    os_, ms, ls = [], [], []
    for (window, rate), qkv in zip(DIL_PAIRS, qkvs):
        n_keys = window // rate
        tab = _band_table(rel_bias, rate, n_keys)
        o, m, l = _band(
            f"dil_attn_r{rate}", qkv.reshape(t, n),
            lambda j: j, lambda j: groups + j, lambda j: 2 * groups + j,
            groups, npairs, 1, tab, lambda j: j, seq_len=t // rate)
        os_.append(o)
        ms.append(m)
        ls.append(l)
    expand = jnp.repeat(jnp.eye(LANES, DIL_HEADS, dtype=BF16), DIL_HEAD_DIM, axis=1)
    return _dil_out(os_, ms, ls, rates, expand, w_o.astype(BF16), h)


def _fox(h, gain, w_in, b_f, w_o, tm=1024, tn=1024):
    t, dm = h.shape
    tm = min(tm, t)
    hh = FOX_HEADS
    hd = hh * FOX_HEAD_DIM
    heads_per_tile = tn // FOX_HEAD_DIM
    colscale = jnp.concatenate([jnp.full((hd,), FOX_HEAD_DIM ** -0.5 * LOG2E, F32), jnp.ones((hd,), F32)])[None]
    w_all = jnp.pad(w_in, ((0, 0), (0, 3 * hd + LANES - w_in.shape[1]))).astype(BF16)

    def qk_epilogue(y, extras, outs):
        (sc_ref,) = extras
        (o_ref,) = outs
        y = y * sc_ref[...]
        for a in range(heads_per_tile):
            o_ref[a] = y[:, a * FOX_HEAD_DIM:(a + 1) * FOX_HEAD_DIM].astype(BF16)

    (qk,) = _proj(
        "fox_qk", h, gain, w_all,
        extras=[(colscale, pl.BlockSpec((1, tn), lambda i, j: (0, j)))],
        outs=[((2 * hh, t, FOX_HEAD_DIM), BF16,
               pl.BlockSpec((heads_per_tile, tm, FOX_HEAD_DIM), lambda i, j: (j, i, 0)))],
        epilogue=qk_epilogue, tm=tm, tn=tn, col0=0, n=2 * hd)

    def v_epilogue(y, extras, outs):
        (o_ref,) = outs
        for a in range(heads_per_tile):
            o_ref[a, :, :LANES] = y[:, a * FOX_HEAD_DIM:(a + 1) * FOX_HEAD_DIM].astype(BF16)
            o_ref[a, :, LANES:] = jnp.ones((tm, LANES), BF16)

    (v_aug,) = _proj(
        "fox_v", h, gain, w_all,
        extras=[],
        outs=[((hh, t, 2 * LANES), BF16, pl.BlockSpec((heads_per_tile, tm, 2 * LANES), lambda i, j: (j, i, 0)))],
        epilogue=v_epilogue, tm=tm, tn=tn, col0=2 * hd, n=hd)

    def gate_epilogue(y, extras, outs):
        (b_ref,) = extras
        (o_ref,) = outs
        x = y + b_ref[...]
        o_ref[...] = jnp.minimum(x, 0.0) - jnp.log(1.0 + jnp.exp(-jnp.abs(x)))

    b_gate = jnp.pad(b_f.astype(F32), (0, LANES - hh))[None]
    (log_f,) = _proj(
        "fox_gate", h, gain, w_all,
        extras=[(b_gate, pl.BlockSpec((1, LANES), lambda i, j: (0, 0)))],
        outs=[((t, LANES), F32, pl.BlockSpec((tm, LANES), lambda i, j: (i, 0)))],
        epilogue=gate_epilogue, tm=tm, tn=LANES, col0=3 * hd, n=LANES)
    cum, cum_t = _cumsum(log_f)
    o = _flash("fox_attn", qk, qk, v_aug, 0, hh, 0, hh, cum=cum, cumt=cum_t)
    return _out_proj("fox_out", o, w_o.astype(BF16), jnp.zeros((1, dm), F32), h)


def kernel(x, positions, rel_bias, norm_mix, norm_ffn, mla_w_in, mla_g_q, mla_g_kv, mla_w_qb, mla_w_kvb, mla_w_o, swa_w_qkv, swa_b_qkv, swa_sinks, swa_w_o, swa_b_o, dil_w_qkv, dil_w_o, fox_w_in, fox_b_f, fox_w_o, ffn_w_in, ffn_conv_w, ffn_conv_b, ffn_w_out, final_norm):
    bsz, t, dm = x.shape
    assert bsz == 1
    depth = norm_mix.shape[0]
    h = x.reshape(t, dm)
    pos = positions.reshape(t)
    ffn_w_in_b = ffn_w_in.astype(BF16)
    ffn_w_out_b = ffn_w_out.astype(BF16)
    for i in range(depth):
        kind = i % 4
        r = i // 4
        if kind == 0:
            h = _mla(h, norm_mix[i], pos, mla_w_in[r], mla_g_q[r], mla_g_kv[r], mla_w_qb[r], mla_w_kvb[r], mla_w_o[r])
        elif kind == 1:
            h = _swa(h, norm_mix[i], rel_bias, swa_w_qkv[r], swa_b_qkv[r], swa_sinks[r], swa_w_o[r], swa_b_o[r])
        elif kind == 2:
            h = _dilated(h, norm_mix[i], rel_bias, dil_w_qkv[r], dil_w_o[r])
        else:
            h = _fox(h, norm_mix[i], fox_w_in[r], fox_b_f[r], fox_w_o[r])
        h = _ffn(h, norm_ffn[i], i, ffn_w_in_b, ffn_conv_w, ffn_conv_b, ffn_w_out_b, final_norm,
                 final=(i == depth - 1))
    return h.reshape(bsz, t, dm)
```
